```python
import math
import jax, jax.numpy as jnp
from jax import lax
import numpy as np

D_MODEL = 1024
BATCH = 8
SEQ = 2048
DEPTH = 1
DEC_BATCH = 128
DEC_SEQ = 4
PAST_LEN = 8192
PAGE_SIZE = 128

HEAD_DIM = 64
N_HEADS_A = 8
N_KV_A = 2
GROUP_A = N_HEADS_A // N_KV_A
N_IDX_HEADS = 8
IDX_DIM = 64
TOPK_MAX = 256
N_HEADS_B = 8
N_BUCKETS = 32
MAX_DISTANCE = 128
D_FF = 2816
Q_BLOCK = 128
EPS = 1e-6

WIDTH_A = N_HEADS_A * HEAD_DIM
KV_WIDTH_A = N_KV_A * HEAD_DIM
WIDTH_IDX = N_IDX_HEADS * IDX_DIM
WIDTH_B = N_HEADS_B * HEAD_DIM
D_IN = WIDTH_A + 2 * KV_WIDTH_A + WIDTH_IDX + IDX_DIM + N_IDX_HEADS + 3 * WIDTH_B + 2 * D_MODEL
ATTN_SCALE = HEAD_DIM ** -0.5
IDX_SCALE = IDX_DIM ** -0.5
IDX_HEAD_SCALE = N_IDX_HEADS ** -0.5

kernel_name = "hybrid_dsa_stickbreaking_macaron_step"


def rmsnorm(x, g):
    xf = x.astype(jnp.float32)
    r = lax.rsqrt(jnp.mean(xf * xf, axis=-1, keepdims=True) + EPS)
    return (xf * r).astype(x.dtype) * g


def swiglu(x, w_up, w_down):
    gate, up = jnp.split(x @ w_up, 2, axis=-1)
    return (jax.nn.silu(gate) * up) @ w_down


def t5_bucket(dist):
    max_exact = N_BUCKETS // 2
    d = jnp.maximum(dist, 0)
    large = max_exact + (jnp.log(jnp.maximum(d, 1).astype(jnp.float32) / max_exact)
                         / math.log(MAX_DISTANCE / max_exact) * (N_BUCKETS - max_exact)).astype(jnp.int32)
    large = jnp.minimum(large, N_BUCKETS - 1)
    return jnp.where(d < max_exact, d, large)


def split_projection(p):
    B, S = p.shape[:2]
    sizes = (WIDTH_A, KV_WIDTH_A, KV_WIDTH_A, WIDTH_IDX, IDX_DIM, N_IDX_HEADS,
             WIDTH_B, WIDTH_B, WIDTH_B, D_MODEL, D_MODEL)
    offs = np.cumsum((0,) + sizes)
    q_a, k_a, v_a, q_i, k_i, w_i, q_b, k_b, v_b, g_a, g_b = [
        p[..., int(offs[i]):int(offs[i + 1])] for i in range(len(sizes))]
    return (q_a.reshape(B, S, N_KV_A, GROUP_A, HEAD_DIM),
            k_a.reshape(B, S, N_KV_A, HEAD_DIM),
            v_a.reshape(B, S, N_KV_A, HEAD_DIM),
            q_i.reshape(B, S, N_IDX_HEADS, IDX_DIM),
            k_i, w_i,
            q_b.reshape(B, S, N_HEADS_B, HEAD_DIM),
            k_b.reshape(B, S, N_HEADS_B, HEAD_DIM),
            v_b.reshape(B, S, N_HEADS_B, HEAD_DIM),
            g_a, g_b)


def gather_rows(a, idx):
    return jax.vmap(lambda ai, ii: ai[ii])(a, idx)


def indexer_topk(q_idx, w_idx, k_idx, q_pos, topk):
    s = jax.nn.relu(jnp.einsum('bthd,bsd->bths', q_idx, k_idx).astype(jnp.float32) * IDX_SCALE)
    score = jnp.einsum('bth,bths->bts', w_idx.astype(jnp.float32) * IDX_HEAD_SCALE, s)
    key_pos = jnp.arange(k_idx.shape[1])
    score = jnp.where(key_pos[None, None, :] <= q_pos[None, :, None], score, -jnp.inf)
    return lax.top_k(score, topk)[1]


def sparse_attend(q, k_sel, v_sel, sel, q_pos, rel_bias):
    dist = q_pos[None, :, None] - sel
    logits = jnp.einsum('bthgd,btkhd->bthgk', q, k_sel).astype(jnp.float32) * ATTN_SCALE
    bias = rel_bias[t5_bucket(dist)].reshape(dist.shape + (N_KV_A, GROUP_A))
    bias = jnp.moveaxis(bias, 2, -1).astype(jnp.float32)
    logits = jnp.where((dist >= 0)[:, :, None, None, :], logits + bias, -jnp.inf)
    p = jax.nn.softmax(logits, axis=-1)
    out = jnp.einsum('bthgk,btkhd->bthgd', p.astype(v_sel.dtype), v_sel)
    return out.reshape(q.shape[0], q.shape[1], WIDTH_A)


def sb_weights(z, mask, prior):
    log1m = jnp.where(mask, jax.nn.log_sigmoid(-z), 0.0)
    after = lax.cumsum(log1m, axis=log1m.ndim - 1, reverse=True) - log1m + prior[..., None]
    a = jnp.where(mask, jnp.exp(jax.nn.log_sigmoid(z) + after), 0.0)
    return a, jnp.sum(log1m, axis=-1)


def dsa_prompt(q, k, v, q_idx, k_idx, w_idx, rel_bias):
    B, S = q.shape[:2]
    topk = min(TOPK_MAX, S // 4)

    def block(i):
        q0 = i * Q_BLOCK
        sl = lambda a: lax.dynamic_slice_in_dim(a, q0, Q_BLOCK, axis=1)
        q_pos = q0 + jnp.arange(Q_BLOCK)
        sel = indexer_topk(sl(q_idx), sl(w_idx), k_idx, q_pos, topk)
        return sparse_attend(sl(q), gather_rows(k, sel), gather_rows(v, sel), sel, q_pos, rel_bias)

    out = lax.map(block, jnp.arange(S // Q_BLOCK))
    return jnp.moveaxis(out, 0, 1).reshape(B, S, WIDTH_A)


def sb_prompt(q, k, v):
    B, S = q.shape[:2]
    key_pos = jnp.arange(S)

    def block(i):
        q0 = i * Q_BLOCK
        qb = lax.dynamic_slice_in_dim(q, q0, Q_BLOCK, axis=1)
        q_pos = q0 + jnp.arange(Q_BLOCK)
        z = jnp.einsum('bthd,bshd->bhts', qb, k).astype(jnp.float32) * ATTN_SCALE
        mask = key_pos[None, :] < q_pos[:, None]
        a, _ = sb_weights(z, mask, jnp.zeros(z.shape[:-1], jnp.float32))
        return jnp.einsum('bhts,bshd->bthd', a.astype(v.dtype), v)

    out = lax.map(block, jnp.arange(S // Q_BLOCK))
    return jnp.moveaxis(out, 0, 1).reshape(B, S, WIDTH_B)


def dsa_sample(q, k_new, v_new, q_idx, k_idx_new, w_idx, cache_k_a, cache_v_a, cache_k_idx, page_table, rel_bias, l):
    N, T = q.shape[:2]
    past = page_table.shape[1] * PAGE_SIZE
    topk = min(TOPK_MAX, (past + T) // 4)
    kidx_past = cache_k_idx[l, page_table].reshape(N, past, IDX_DIM)
    kidx_all = jnp.concatenate([kidx_past, k_idx_new.astype(kidx_past.dtype)], axis=1)
    q_pos = past + jnp.arange(T)
    sel = indexer_topk(q_idx, w_idx, kidx_all, q_pos, topk)
    is_past = (sel < past)[..., None, None]
    ps = jnp.minimum(sel, past - 1)
    phys = jnp.take_along_axis(page_table, (ps // PAGE_SIZE).reshape(N, -1), axis=1).reshape(sel.shape)
    off = ps % PAGE_SIZE
    ns = jnp.clip(sel - past, 0, T - 1)
    k_sel = jnp.where(is_past, cache_k_a[l, phys, off], gather_rows(k_new, ns))
    v_sel = jnp.where(is_past, cache_v_a[l, phys, off], gather_rows(v_new, ns))
    return sparse_attend(q, k_sel, v_sel, sel, q_pos, rel_bias)


def sb_sample(q, k_new, v_new, cache_k_b, cache_v_b, page_table, l):
    N, T = q.shape[:2]
    pos = jnp.arange(T)
    mask_new = pos[None, :] < pos[:, None]
    z = jnp.einsum('nthd,nshd->nhts', q, k_new).astype(jnp.float32) * ATTN_SCALE
    a, acc = sb_weights(z, mask_new, jnp.zeros(z.shape[:-1], jnp.float32))
    out = jnp.einsum('nhts,nshd->nthd', a, v_new.astype(jnp.float32))
    mask_page = jnp.ones((T, PAGE_SIZE), bool)

    def step(carry, phys):
        acc, out = carry
        kp = cache_k_b[l, phys]
        vp = cache_v_b[l, phys]
        zp = jnp.einsum('nthd,nshd->nhts', q, kp.astype(q.dtype)).astype(jnp.float32) * ATTN_SCALE
        ap, s = sb_weights(zp, mask_page, acc)
        out = out + jnp.einsum('nhts,nshd->nthd', ap, vp.astype(jnp.float32))
        return (acc + s, out), None

    (acc, out), _ = lax.scan(step, (acc, out), jnp.flip(page_table.T, axis=0))
    return out.astype(q.dtype).reshape(N, T, WIDTH_B)


def gated_merge(out_a, out_b, g_a, g_b, w_o_a, w_o_b, w_out):
    return (jax.nn.sigmoid(g_a) * (out_a @ w_o_a) + jax.nn.sigmoid(g_b) * (out_b @ w_o_b)) @ w_out


def setup_inputs(seed: int = 0) -> dict:
    key = jax.random.key(seed)
    ks = jax.random.split(key, 24)
    n_pages = PAST_LEN // PAGE_SIZE
    n_pool = (DEC_BATCH * n_pages * 5) // 4
    nrm = lambda k, shape, scale: jax.random.normal(k, shape, jnp.float32) * scale
    page_table = jax.random.permutation(ks[0], n_pool)[:DEC_BATCH * n_pages].reshape(DEC_BATCH, n_pages).astype(jnp.int32)
    return {
        "x_prompt": nrm(ks[1], (BATCH, SEQ, D_MODEL), 1.0),
        "x_sample": nrm(ks[2], (DEC_BATCH, DEC_SEQ, D_MODEL), 1.0),
        "cache_k_a": nrm(ks[3], (DEPTH, n_pool, PAGE_SIZE, N_KV_A, HEAD_DIM), 1.0),
        "cache_v_a": nrm(ks[4], (DEPTH, n_pool, PAGE_SIZE, N_KV_A, HEAD_DIM), 1.0),
        "cache_k_idx": nrm(ks[5], (DEPTH, n_pool, PAGE_SIZE, IDX_DIM), 1.0),
        "cache_k_b": nrm(ks[6], (DEPTH, n_pool, PAGE_SIZE, N_HEADS_B, HEAD_DIM), 1.0),
        "cache_v_b": nrm(ks[7], (DEPTH, n_pool, PAGE_SIZE, N_HEADS_B, HEAD_DIM), 1.0),
        "page_table": page_table,
        "w_in": nrm(ks[8], (DEPTH, D_MODEL, D_IN), D_MODEL ** -0.5),
        "w_o_a": nrm(ks[9], (DEPTH, WIDTH_A, D_MODEL), WIDTH_A ** -0.5),
        "w_o_b": nrm(ks[10], (DEPTH, WIDTH_B, D_MODEL), WIDTH_B ** -0.5),
        "w_out": nrm(ks[11], (DEPTH, D_MODEL, D_MODEL), D_MODEL ** -0.5),
        "rel_bias": nrm(ks[12], (N_BUCKETS, N_HEADS_A), 0.5),
        "g_ffn1": 1.0 + nrm(ks[13], (DEPTH, D_MODEL), 0.01),
        "w_up1": nrm(ks[14], (DEPTH, D_MODEL, 2 * D_FF), D_MODEL ** -0.5),
        "w_down1": nrm(ks[15], (DEPTH, D_FF, D_MODEL), D_FF ** -0.5),
        "g_mix": 1.0 + nrm(ks[16], (DEPTH, D_MODEL), 0.01),
        "g_ffn2": 1.0 + nrm(ks[17], (DEPTH, D_MODEL), 0.01),
        "w_up2": nrm(ks[18], (DEPTH, D_MODEL, 2 * D_FF), D_MODEL ** -0.5),
        "w_down2": nrm(ks[19], (DEPTH, D_FF, D_MODEL), D_FF ** -0.5),
        "g_final": 1.0 + nrm(ks[20], (D_MODEL,), 0.01),
    }


def reference(x_prompt, x_sample, cache_k_a, cache_v_a, cache_k_idx, cache_k_b, cache_v_b, page_table,
              w_in, w_o_a, w_o_b, w_out, rel_bias, g_ffn1, w_up1, w_down1, g_mix, g_ffn2, w_up2, w_down2, g_final):
    xp, xs = x_prompt, x_sample
    rows_p = {n: [] for n in ('ka', 'va', 'ki', 'kb', 'vb')}
    rows_s = {n: [] for n in ('ka', 'va', 'ki', 'kb', 'vb')}
    for l in range(DEPTH):
        xp = xp + 0.5 * swiglu(rmsnorm(xp, g_ffn1[l]), w_up1[l], w_down1[l])
        xs = xs + 0.5 * swiglu(rmsnorm(xs, g_ffn1[l]), w_up1[l], w_down1[l])

        qa, ka, va, qi, ki, wi, qb, kb, vb, ga, gb = split_projection(rmsnorm(xp, g_mix[l]) @ w_in[l])
        oa = dsa_prompt(qa, ka, va, qi, ki, wi, rel_bias)
        ob = sb_prompt(qb, kb, vb)
        xp = xp + gated_merge(oa, ob, ga, gb, w_o_a[l], w_o_b[l], w_out[l])
        for n, r in zip(('ka', 'va', 'ki', 'kb', 'vb'), (ka, va, ki, kb, vb)):
            rows_p[n].append(r)

        qa, ka, va, qi, ki, wi, qb, kb, vb, ga, gb = split_projection(rmsnorm(xs, g_mix[l]) @ w_in[l])
        oa = dsa_sample(qa, ka, va, qi, ki, wi, cache_k_a, cache_v_a, cache_k_idx, page_table, rel_bias, l)
        ob = sb_sample(qb, kb, vb, cache_k_b, cache_v_b, page_table, l)
        xs = xs + gated_merge(oa, ob, ga, gb, w_o_a[l], w_o_b[l], w_out[l])
        for n, r in zip(('ka', 'va', 'ki', 'kb', 'vb'), (ka, va, ki, kb, vb)):
            rows_s[n].append(r)

        xp = xp + 0.5 * swiglu(rmsnorm(xp, g_ffn2[l]), w_up2[l], w_down2[l])
        xs = xs + 0.5 * swiglu(rmsnorm(xs, g_ffn2[l]), w_up2[l], w_down2[l])

    y_prompt = rmsnorm(xp, g_final)
    y_sample = rmsnorm(xs, g_final)
    k_a_prompt = jnp.stack(rows_p['ka'])
    v_a_prompt = jnp.stack(rows_p['va'])
    k_idx_prompt = jnp.stack(rows_p['ki'])
    k_b_prompt = jnp.stack(rows_p['kb'])
    v_b_prompt = jnp.stack(rows_p['vb'])
    k_a_sample = jnp.stack(rows_s['ka'])
    v_a_sample = jnp.stack(rows_s['va'])
    k_idx_sample = jnp.stack(rows_s['ki'])
    k_b_sample = jnp.stack(rows_s['kb'])
    v_b_sample = jnp.stack(rows_s['vb'])
    return (y_prompt, y_sample, k_a_prompt, v_a_prompt, k_idx_prompt, k_b_prompt, v_b_prompt,
            k_a_sample, v_a_sample, k_idx_sample, k_b_sample, v_b_sample)
```

```python
import functools
import math

import jax
import jax.numpy as jnp
import numpy as np
from jax import lax
from jax.experimental import pallas as pl
from jax.experimental.pallas import tpu as pltpu

F32 = jnp.float32
BF16 = jnp.bfloat16
I32 = jnp.int32

HEAD_DIM = 64
IDX_DIM = 64
N_HEADS_A = 8
N_KV_A = 2
GROUP_A = N_HEADS_A // N_KV_A
N_IDX_HEADS = 8
N_HEADS_B = 8
TOPK_MAX = 256
N_BUCKETS = 32
MAX_DISTANCE = 128
EPS = 1e-6
ATTN_SCALE = HEAD_DIM ** -0.5
IDX_SCALE = IDX_DIM ** -0.5
IDX_HEAD_SCALE = N_IDX_HEADS ** -0.5

LANES = 128
Q_TILE = 128
NEG = -1e30
INT_MIN = -2 ** 31
VMEM_LIMIT = 56 * 1024 * 1024
FFN_CHUNK = 256
TOKEN_TILE = 512


def _pick_tile(n, pref):
    t = min(n, pref)
    while n % t or t % 8:
        t -= 1
    return t


def _const_spec(shape):
    nd = len(shape)
    return pl.BlockSpec(shape, lambda *_: (0,) * nd, pipeline_mode=pl.Buffered(1))


def _dot(a, b):
    return jnp.dot(a, b, preferred_element_type=F32)


def _dot_nt(a, b):
    return lax.dot_general(a, b, (((1,), (1,)), ((), ())), preferred_element_type=F32)


def _rms(x, g):
    r = lax.rsqrt(jnp.mean(x * x, axis=-1, keepdims=True) + EPS)
    return (x * r) * g


def _softplus(z):
    return jnp.maximum(z, 0.0) + jnp.log(1.0 + jnp.exp(-jnp.abs(z)))


def _split_bf16(x):
    hi = x.astype(BF16)
    lo = (x - hi.astype(F32)).astype(BF16)
    return hi, lo


def _sortable(x):
    bits = lax.bitcast_convert_type(x + 0.0, I32)
    return bits ^ ((bits >> 31) & jnp.int32(0x7FFFFFFF))


def _swiglu(x, g_ref, wup_ref, wdn_ref, act_ref):
    d_ff = wdn_ref.shape[0]
    h = _rms(x, g_ref[...]).astype(BF16)
    for c in range(d_ff // FFN_CHUNK):
        lo = c * FFN_CHUNK
        gate = _dot(h, wup_ref[:, lo:lo + FFN_CHUNK])
        up = _dot(h, wup_ref[:, d_ff + lo:d_ff + lo + FFN_CHUNK])
        act_ref[:, lo:lo + FFN_CHUNK] = (gate * jax.nn.sigmoid(gate) * up).astype(BF16)
    return _dot(act_ref[...], wdn_ref[...])


def _ffn_kernel(x_ref, g_ref, wup_ref, wdn_ref, o_ref, act_ref):
    x = x_ref[...]
    o_ref[...] = x + 0.5 * _swiglu(x, g_ref, wup_ref, wdn_ref, act_ref)


def _ffn_call(x, g, wup, wdn):
    n, d = x.shape
    d_ff = wdn.shape[0]
    tm = _pick_tile(n, TOKEN_TILE)
    row = lambda w: pl.BlockSpec((tm, w), lambda i: (i, 0))
    return pl.pallas_call(
        _ffn_kernel,
        grid=(n // tm,),
        in_specs=[row(d), _const_spec(g.shape), _const_spec(wup.shape), _const_spec(wdn.shape)],
        out_specs=row(d),
        out_shape=jax.ShapeDtypeStruct((n, d), F32),
        scratch_shapes=[pltpu.VMEM((tm, d_ff), BF16)],
        compiler_params=pltpu.CompilerParams(dimension_semantics=("arbitrary",), vmem_limit_bytes=VMEM_LIMIT),
        name="ffn_pre",
    )(x, g, wup, wdn)


def _proj_kernel(x_ref, g_ref, wqa, wka, wva, wqi, wkw, wkk, wqb, wkb, wvb, wga, wgb,
                 qa, ka, va, kab, vab, qi, kw, kk, qb, kb, vb, kbb, vbb, sa, sb):
    h = _rms(x_ref[...], g_ref[...]).astype(BF16)
    qa[...] = _dot(h, wqa[...]).astype(BF16)
    k = _dot(h, wka[...])
    ka[...] = k
    kab[...] = k.astype(BF16)
    v = _dot(h, wva[...])
    va[...] = v
    vab[...] = v.astype(BF16)
    qi[...] = _dot(h, wqi[...]).astype(BF16)
    kw[...] = _dot(h, wkw[...])
    kk[...] = _dot(h, wkk[...]).astype(BF16)
    qb[...] = _dot(h, wqb[...]).astype(BF16)
    k = _dot(h, wkb[...])
    kb[...] = k
    kbb[...] = k.astype(BF16)
    v = _dot(h, wvb[...])
    vb[...] = v
    vbb[...] = v.astype(BF16)
    sa[...] = jax.nn.sigmoid(_dot(h, wga[...]))
    sb[...] = jax.nn.sigmoid(_dot(h, wgb[...]))


_PROJ_OUT = (("qa", 512, BF16), ("ka", 128, F32), ("va", 128, F32), ("kab", 128, BF16), ("vab", 128, BF16),
             ("qi", 512, BF16), ("kw", 128, F32), ("kk", 128, BF16), ("qb", 512, BF16),
             ("kb", 512, F32), ("vb", 512, F32), ("kbb", 512, BF16), ("vbb", 512, BF16),
             ("sa", 1024, F32), ("sb", 1024, F32))


def _proj_call(x, g, weights):
    n, d = x.shape
    tm = _pick_tile(n, TOKEN_TILE)
    row = lambda w: pl.BlockSpec((tm, w), lambda i: (i, 0))
    outs = pl.pallas_call(
        _proj_kernel,
        grid=(n // tm,),
        in_specs=[row(d), _const_spec(g.shape)] + [_const_spec(w.shape) for w in weights],
        out_specs=[row(w) for _, w, _ in _PROJ_OUT],
        out_shape=[jax.ShapeDtypeStruct((n, w), dt) for _, w, dt in _PROJ_OUT],
        compiler_params=pltpu.CompilerParams(dimension_semantics=("arbitrary",), vmem_limit_bytes=VMEM_LIMIT),
        name="proj",
    )(x, g, *weights)
    return dict(zip([nm for nm, _, _ in _PROJ_OUT], outs))


def _post_kernel(x_ref, oa_ref, ob_ref, sa_ref, sb_ref, woa, wob, wout, g2_ref, wup_ref, wdn_ref, gf_ref,
                 y_ref, act_ref):
    mix = sa_ref[...] * _dot(oa_ref[...], woa[...]) + sb_ref[...] * _dot(ob_ref[...], wob[...])
    x2 = x_ref[...] + _dot(mix.astype(BF16), wout[...])
    x3 = x2 + 0.5 * _swiglu(x2, g2_ref, wup_ref, wdn_ref, act_ref)
    y_ref[...] = _rms(x3, gf_ref[...])


def _post_call(x, oa, ob, sa, sb, woa, wob, wout, g2, wup, wdn, gf):
    n, d = x.shape
    d_ff = wdn.shape[0]
    tm = _pick_tile(n, TOKEN_TILE)
    row = lambda w: pl.BlockSpec((tm, w), lambda i: (i, 0))
    consts = (woa, wob, wout, g2, wup, wdn, gf)
    return pl.pallas_call(
        _post_kernel,
        grid=(n // tm,),
        in_specs=[row(d), row(oa.shape[1]), row(ob.shape[1]), row(d), row(d)] + [_const_spec(c.shape) for c in consts],
        out_specs=row(d),
        out_shape=jax.ShapeDtypeStruct((n, d), F32),
        scratch_shapes=[pltpu.VMEM((tm, d_ff), BF16)],
        compiler_params=pltpu.CompilerParams(dimension_semantics=("arbitrary",), vmem_limit_bytes=VMEM_LIMIT),
        name="post",
    )(x, oa, ob, sa, sb, *consts)


def _t5_bucket_np(dist):
    max_exact = N_BUCKETS // 2
    d = np.maximum(dist, 0)
    ratio = np.maximum(d, 1).astype(np.float32) / np.float32(max_exact)
    large = max_exact + (np.log(ratio) / np.float32(math.log(MAX_DISTANCE / max_exact))
                         * np.float32(N_BUCKETS - max_exact)).astype(np.int32)
    large = np.minimum(large, N_BUCKETS - 1)
    return np.where(d < max_exact, d, large).astype(np.int32)


def _bias_kernel(rel_ref, idx_ref, out_ref):
    idx = idx_ref[...]
    for h in range(out_ref.shape[0]):
        acc = jnp.zeros(idx.shape, F32)
        for b in range(N_BUCKETS):
            acc = jnp.where(idx == b, rel_ref[b, h], acc)
        out_ref[h] = acc


def _bias_call(rel_bias, idx):
    return pl.pallas_call(
        _bias_kernel,
        in_specs=[pl.BlockSpec(memory_space=pltpu.SMEM), pl.BlockSpec(memory_space=pltpu.VMEM)],
        out_specs=pl.BlockSpec(memory_space=pltpu.VMEM),
        out_shape=jax.ShapeDtypeStruct((rel_bias.shape[1],) + idx.shape, F32),
        name="rel_bias_table",
    )(rel_bias, jnp.asarray(idx))


def _half_masks():
    lane = lax.broadcasted_iota(I32, (Q_TILE, LANES), 1)
    return lane < HEAD_DIM


def _store_masked_pairs(src_ref, dst_ref, n_pairs):
    lo_half = _half_masks()
    for p in range(n_pairs):
        pair = src_ref[:, p * LANES:(p + 1) * LANES].astype(F32)
        dst_ref[p, 0:Q_TILE, :] = jnp.where(lo_half, pair, 0.0).astype(BF16)
        dst_ref[p, Q_TILE:2 * Q_TILE, :] = jnp.where(lo_half, 0.0, pair).astype(BF16)


def _block_diag_values(vt, with_ones):
    lo_half = _half_masks()
    v = vt.astype(F32)
    top = jnp.where(lo_half, v, 0.0)
    bot = jnp.where(lo_half, 0.0, v)
    if with_ones:
        top = jnp.concatenate([top, jnp.where(lo_half, 1.0, 0.0)], axis=1)
        bot = jnp.concatenate([bot, jnp.where(lo_half, 0.0, 1.0)], axis=1)
    return jnp.concatenate([top, bot], axis=0).astype(BF16)


def _tri_ones(strict_upper):
    r = np.arange(LANES)
    if strict_upper:
        tri = (r[:, None] < r[None, :])
    else:
        tri = (r[:, None] > r[None, :])
    return jnp.asarray(np.concatenate([tri, np.ones((LANES, LANES), bool)], axis=1), dtype=BF16)


def _dsa_prompt_kernel(rel_ref, qi_ref, kw_ref, kk_ref, qa_ref, ka_ref, va_ref, bias_ref, uo_ref, o_ref,
                       qim_ref, qam_ref, wb_ref, key_ref, sel_ref, lg_ref, mx_ref, acc_ref, *, topk):
    i = pl.program_id(1)
    ntile = i + 1
    n_pairs = N_HEADS_A // 2
    row = lax.broadcasted_iota(I32, (Q_TILE, LANES), 0)
    lane = lax.broadcasted_iota(I32, (Q_TILE, LANES), 1)
    causal_diag = lane <= row

    _store_masked_pairs(qi_ref, qim_ref, N_IDX_HEADS // 2)
    _store_masked_pairs(qa_ref, qam_ref, n_pairs)
    w = kw_ref[:, IDX_DIM:IDX_DIM + N_IDX_HEADS] * IDX_HEAD_SCALE
    for h in range(N_IDX_HEADS):
        wb_ref[h] = jnp.broadcast_to(w[:, h:h + 1], (Q_TILE, LANES))
    mx_ref[...] = jnp.full(mx_ref.shape, NEG, F32)
    acc_ref[...] = jnp.zeros(acc_ref.shape, F32)

    def key_rows(j):
        return pl.ds(pl.multiple_of(j * LANES, LANES), LANES)

    def score_tile(j, c):
        kt = kk_ref[key_rows(j), :]
        acc = jnp.zeros((Q_TILE, LANES), F32)
        for p in range(N_IDX_HEADS // 2):
            s = _dot_nt(qim_ref[p], kt)
            acc = acc + wb_ref[2 * p] * jnp.maximum(s[:Q_TILE] * IDX_SCALE, 0.0)
            acc = acc + wb_ref[2 * p + 1] * jnp.maximum(s[Q_TILE:] * IDX_SCALE, 0.0)
        valid = jnp.logical_or(j < i, causal_diag)
        key_ref[j] = jnp.where(valid, _sortable(acc), INT_MIN)
        return c
    lax.fori_loop(0, ntile, score_tile, 0)

    def count_tiles(pred):
        def body(j, cnt):
            return cnt + jnp.where(pred(key_ref[j]), 1, 0)
        cnt = lax.fori_loop(0, ntile, body, jnp.zeros((Q_TILE, LANES), I32))
        return jnp.sum(cnt, axis=1, keepdims=True)

    def bit_step(it, u):
        bit = lax.shift_left(jnp.int32(1), 31 - it)
        cand = (u | bit) ^ INT_MIN
        tot = count_tiles(lambda k: k >= cand)
        return jnp.where(tot >= topk, u | bit, u)
    thr = lax.fori_loop(0, 32, bit_step, jnp.zeros((Q_TILE, LANES), I32)) ^ INT_MIN
    need = (topk - count_tiles(lambda k: k > thr)).astype(F32)

    def select_tile(j, carry):
        key = key_ref[j]
        eq = key == thr
        cs = _dot(jnp.where(eq, 1.0, 0.0).astype(BF16), uo_ref[...])
        take = jnp.logical_or(key > thr, jnp.logical_and(eq, cs[:, :LANES] + carry < need))
        take = jnp.logical_and(take, jnp.logical_or(j < i, causal_diag))
        sel_ref[j] = jnp.where(take, 0.0, NEG)
        return carry + cs[:, LANES:]
    lax.fori_loop(0, ntile, select_tile, jnp.zeros((Q_TILE, LANES), F32))

    def logit_tile(j, mode):
        kt = ka_ref[key_rows(j), :]
        sel = sel_ref[j]
        for p in range(n_pairs):
            s = _dot_nt(qam_ref[p], kt) * ATTN_SCALE
            for half, h in ((0, p), (1, GROUP_A + p)):
                if mode == 0:
                    b = rel_ref[N_BUCKETS - 1, h]
                elif mode == 1:
                    b = bias_ref[h, :, 0:LANES]
                else:
                    b = bias_ref[h, :, LANES:2 * LANES]
                lg = s[half * Q_TILE:(half + 1) * Q_TILE] + b + sel
                hh = 2 * p + half
                lg_ref[hh, j] = lg
                mx_ref[hh] = jnp.maximum(mx_ref[hh], lg)

    def far_tile(j, c):
        logit_tile(j, 0)
        return c
    lax.fori_loop(0, jnp.maximum(i - 1, 0), far_tile, 0)

    @pl.when(i >= 1)
    def _():
        logit_tile(i - 1, 1)
    logit_tile(i, 2)

    for hh in range(N_HEADS_A):
        m = jnp.max(mx_ref[hh], axis=1, keepdims=True)
        mx_ref[hh] = jnp.broadcast_to(m, (Q_TILE, LANES))

    def value_tile(j, c):
        rhs = _block_diag_values(va_ref[key_rows(j), :], True)
        for p in range(n_pairs):
            p0 = jnp.exp(lg_ref[2 * p, j] - mx_ref[2 * p])
            p1 = jnp.exp(lg_ref[2 * p + 1, j] - mx_ref[2 * p + 1])
            pr = jnp.concatenate([p0, p1], axis=1).astype(BF16)
            acc_ref[p] += _dot(pr, rhs)
        return c
    lax.fori_loop(0, ntile, value_tile, 0)

    for p in range(n_pairs):
        a = acc_ref[p]
        o_ref[:, p * LANES:(p + 1) * LANES] = (a[:, :LANES] / a[:, LANES:]).astype(BF16)


def _dsa_prompt_call(rel_bias, pr, bias_tile, uo, batch, seq):
    nq = seq // Q_TILE
    topk = min(TOPK_MAX, seq // 4)
    qblk = lambda w: pl.BlockSpec((Q_TILE, w), lambda b, i: (b * nq + i, 0))
    kblk = lambda w: pl.BlockSpec((seq, w), lambda b, i: (b, 0))
    return pl.pallas_call(
        functools.partial(_dsa_prompt_kernel, topk=topk),
        grid=(batch, nq),
        in_specs=[pl.BlockSpec(memory_space=pltpu.SMEM),
                  qblk(512), qblk(LANES), kblk(LANES), qblk(512), kblk(LANES), kblk(LANES),
                  _const_spec(bias_tile.shape), _const_spec(uo.shape)],
        out_specs=qblk(512),
        out_shape=jax.ShapeDtypeStruct((batch * seq, 512), BF16),
        scratch_shapes=[
            pltpu.VMEM((N_IDX_HEADS // 2, 2 * Q_TILE, LANES), BF16),
            pltpu.VMEM((N_HEADS_A // 2, 2 * Q_TILE, LANES), BF16),
            pltpu.VMEM((N_IDX_HEADS, Q_TILE, LANES), F32),
            pltpu.VMEM((nq, Q_TILE, LANES), I32),
            pltpu.VMEM((nq, Q_TILE, LANES), F32),
            pltpu.VMEM((N_HEADS_A, nq, Q_TILE, LANES), F32),
            pltpu.VMEM((N_HEADS_A, Q_TILE, LANES), F32),
            pltpu.VMEM((N_HEADS_A // 2, Q_TILE, 2 * LANES), F32),
        ],
        compiler_params=pltpu.CompilerParams(dimension_semantics=("arbitrary", "arbitrary"),
                                             vmem_limit_bytes=VMEM_LIMIT),
        name="dsa_prompt",
    )(rel_bias, pr["qi"], pr["kw"], pr["kk"], pr["qa"], pr["kab"], pr["vab"], bias_tile, uo)


def _sb_prompt_kernel(qb_ref, kb_ref, vb_ref, to_ref, o_ref, qm_ref, acc_ref, out_ref):
    i = pl.program_id(1)
    n_pairs = N_HEADS_B // 2
    row2 = lax.broadcasted_iota(I32, (2 * Q_TILE, LANES), 0) % Q_TILE
    lane2 = lax.broadcasted_iota(I32, (2 * Q_TILE, LANES), 1)
    strict = lane2 < row2

    _store_masked_pairs(qb_ref, qm_ref, n_pairs)
    acc_ref[...] = jnp.zeros(acc_ref.shape, F32)
    out_ref[...] = jnp.zeros(out_ref.shape, F32)

    def tile(j, diag):
        rows = pl.ds(pl.multiple_of(j * LANES, LANES), LANES)
        for p in range(n_pairs):
            cols = slice(p * LANES, (p + 1) * LANES)
            z = _dot_nt(qm_ref[p], kb_ref[rows, cols]) * ATTN_SCALE
            sp = _softplus(z)
            l1m = jnp.where(strict, -sp, 0.0) if diag else -sp
            hi, lo = _split_bf16(l1m)
            cs = _dot(hi, to_ref[...]) + _dot(lo, to_ref[...])
            a = jnp.exp(z - sp + cs[:, :LANES] + acc_ref[p])
            if diag:
                a = jnp.where(strict, a, 0.0)
            acc_ref[p] += cs[:, LANES:]
            a2 = jnp.concatenate([a[:Q_TILE], a[Q_TILE:]], axis=1).astype(BF16)
            out_ref[p] += _dot(a2, _block_diag_values(vb_ref[rows, cols], False))

    tile(i, True)

    def older(jj, c):
        tile(i - 1 - jj, False)
        return c
    lax.fori_loop(0, i, older, 0)

    for p in range(n_pairs):
        o_ref[:, p * LANES:(p + 1) * LANES] = out_ref[p].astype(BF16)


def _sb_prompt_call(pr, to, batch, seq):
    nq = seq // Q_TILE
    qblk = pl.BlockSpec((Q_TILE, 512), lambda b, i: (b * nq + i, 0))
    kblk = pl.BlockSpec((seq, 512), lambda b, i: (b, 0))
    return pl.pallas_call(
        _sb_prompt_kernel,
        grid=(batch, nq),
        in_specs=[qblk, kblk, kblk, _const_spec(to.shape)],
        out_specs=qblk,
        out_shape=jax.ShapeDtypeStruct((batch * seq, 512), BF16),
        scratch_shapes=[
            pltpu.VMEM((N_HEADS_B // 2, 2 * Q_TILE, LANES), BF16),
            pltpu.VMEM((N_HEADS_B // 2, 2 * Q_TILE, LANES), F32),
            pltpu.VMEM((N_HEADS_B // 2, Q_TILE, LANES), F32),
        ],
        compiler_params=pltpu.CompilerParams(dimension_semantics=("arbitrary", "arbitrary"),
                                             vmem_limit_bytes=VMEM_LIMIT),
        name="sb_prompt",
    )(pr["qb"], pr["kbb"], pr["vbb"], to)


SEQ_PAIR = 2
KEY_CHUNK = 1024


def _page_copy(cache_ref, buf_ref, sem, phys, p, page):
    return pltpu.make_async_copy(cache_ref.at[0, phys], buf_ref.at[:, p * page:(p + 1) * page], sem)


def _start_pages(cache_ref, buf_ref, sem, pt_ref, seq, first_page, n_copy, page):
    for p in range(n_copy):
        _page_copy(cache_ref, buf_ref, sem, pt_ref[seq, first_page + p], p, page).start()


def _wait_pages(cache_ref, buf_ref, sem, n_copy, page):
    for p in range(n_copy):
        _page_copy(cache_ref, buf_ref, sem, 0, p, page).wait()


def _dsa_sample_select_kernel(pt_ref, qi_ref, w_ref, kn_ref, cache_ref, uo_ref, sel_ref,
                              kbuf, sem, key_ref, knew_ref, *, topk, n_pages, page):
    s = pl.program_id(0)
    nstep = pl.num_programs(0)
    slot = s % 2
    past = n_pages * page
    n_tiles = past // LANES + 1
    rows = SEQ_PAIR * 4

    def start(step, sl):
        for g in range(SEQ_PAIR):
            _start_pages(cache_ref, kbuf.at[sl, g], sem.at[sl], pt_ref, step * SEQ_PAIR + g, 0, n_pages, page)

    @pl.when(s == 0)
    def _():
        start(s, slot)

    @pl.when(s + 1 < nstep)
    def _():
        start(s + 1, 1 - slot)

    for g in range(SEQ_PAIR):
        _wait_pages(cache_ref, kbuf.at[slot, g], sem.at[slot], n_pages, page)

    q = qi_ref[...]
    w = w_ref[...] * IDX_HEAD_SCALE
    row = lax.broadcasted_iota(I32, (rows, LANES), 0)
    lane = lax.broadcasted_iota(I32, (rows, LANES), 1)

    def head_sum(sc):
        sc = jnp.maximum(sc * IDX_SCALE, 0.0) * w
        return jnp.sum(sc.reshape(N_IDX_HEADS, rows, sc.shape[-1]), axis=0)

    for c in range(past // KEY_CHUNK):
        parts = []
        for g in range(SEQ_PAIR):
            kc = kbuf[slot, g, :, c * KEY_CHUNK:(c + 1) * KEY_CHUNK].astype(BF16)
            parts.append(head_sum(_dot(q, kc)))
        lane_c = lax.broadcasted_iota(I32, (rows, KEY_CHUNK), 0)
        sc = jnp.where(lane_c < 4, parts[0], parts[1])
        key_ref[:, c * KEY_CHUNK:(c + 1) * KEY_CHUNK] = _sortable(sc)

    parts = []
    for g in range(SEQ_PAIR):
        knew_ref[...] = jnp.zeros(knew_ref.shape, F32)
        knew_ref[0:4, :] = kn_ref[g]
        parts.append(head_sum(_dot_nt(q, knew_ref[...].astype(BF16))))
    sc = jnp.where(row < 4, parts[0], parts[1])
    valid_new = jnp.logical_and(lane <= row % 4, lane < 4)
    key_ref[:, past:past + LANES] = jnp.where(valid_new, _sortable(sc), INT_MIN)

    key = key_ref[...]

    def bit_step(it, u):
        bit = lax.shift_left(jnp.int32(1), 31 - it)
        cand = (u | bit) ^ INT_MIN
        tot = jnp.sum(jnp.where(key >= cand, 1, 0), axis=1, keepdims=True)
        return jnp.where(tot >= topk, u | bit, u)
    thr = lax.fori_loop(0, 32, bit_step, jnp.zeros((rows, 1), I32)) ^ INT_MIN
    need = (topk - jnp.sum(jnp.where(key > thr, 1, 0), axis=1, keepdims=True)).astype(F32)

    eq = jnp.where(key == thr, 1.0, 0.0)
    stack = jnp.concatenate([eq[:, t * LANES:(t + 1) * LANES] for t in range(n_tiles)], axis=0).astype(BF16)
    cs = _dot(stack, uo_ref[...])
    carry = jnp.zeros((rows, LANES), F32)
    for t in range(n_tiles):
        kt = key[:, t * LANES:(t + 1) * LANES]
        pre = cs[t * rows:(t + 1) * rows, :LANES] + carry
        carry = carry + cs[t * rows:(t + 1) * rows, LANES:]
        take = jnp.logical_or(kt > thr, jnp.logical_and(kt == thr, pre < need))
        if t == n_tiles - 1:
            take = jnp.logical_and(take, valid_new)
        sel_ref[:, t * LANES:(t + 1) * LANES] = jnp.where(take, 0.0, NEG)


def _dsa_sample_select_call(page_table, qi64, wcol, ki_new, cache_k_idx, uo, topk):
    n, n_pages = page_table.shape
    page = cache_k_idx.shape[-1]
    past = n_pages * page
    width = past + LANES
    rows = SEQ_PAIR * 4
    return pl.pallas_call(
        functools.partial(_dsa_sample_select_kernel, topk=topk, n_pages=n_pages, page=page),
        grid_spec=pltpu.PrefetchScalarGridSpec(
            num_scalar_prefetch=1,
            grid=(n // SEQ_PAIR,),
            in_specs=[pl.BlockSpec((N_IDX_HEADS * rows, IDX_DIM), lambda s, pt: (s, 0)),
                      pl.BlockSpec((N_IDX_HEADS * rows, 1), lambda s, pt: (s, 0)),
                      pl.BlockSpec((SEQ_PAIR, 4, IDX_DIM), lambda s, pt: (s, 0, 0)),
                      pl.BlockSpec(memory_space=pl.ANY),
                      pl.BlockSpec(uo.shape, lambda s, pt: (0, 0))],
            out_specs=pl.BlockSpec((rows, width), lambda s, pt: (s, 0)),
            scratch_shapes=[pltpu.VMEM((2, SEQ_PAIR, IDX_DIM, past), F32),
                            pltpu.SemaphoreType.DMA((2,)),
                            pltpu.VMEM((rows, width), I32),
                            pltpu.VMEM((LANES, IDX_DIM), F32)]),
        out_shape=jax.ShapeDtypeStruct((n * 4, width), F32),
        compiler_params=pltpu.CompilerParams(dimension_semantics=("arbitrary",), vmem_limit_bytes=VMEM_LIMIT),
        name="dsa_sample_select",
    )(page_table, qi64, wcol, ki_new, cache_k_idx, uo)


def _dsa_sample_attn_kernel(pt_ref, qa_ref, sel_ref, bias_ref, cb_ref, kn_ref, vn_ref, kcache, vcache, o_ref,
                            kbuf, vbuf, sem, lg_ref, new_ref, *, n_pages, page):
    n = pl.program_id(0)
    nseq = pl.num_programs(0)
    slot = n % 2
    past = n_pages * page
    n_chunks = past // KEY_CHUNK
    rows = N_HEADS_A * 4

    def start(seq, sl):
        _start_pages(kcache, kbuf.at[sl], sem.at[0, sl], pt_ref, seq, 0, n_pages, page)
        _start_pages(vcache, vbuf.at[sl], sem.at[1, sl], pt_ref, seq, 0, n_pages, page)

    @pl.when(n == 0)
    def _():
        start(n, slot)

    @pl.when(n + 1 < nseq)
    def _():
        start(n + 1, 1 - slot)

    lo_half = lax.broadcasted_iota(I32, (rows // 2, LANES), 1) < HEAD_DIM
    q = qa_ref[...].astype(F32)
    q32 = jnp.concatenate([jnp.where(lo_half, q, 0.0), jnp.where(lo_half, 0.0, q)], axis=0).astype(BF16)

    r = lax.broadcasted_iota(I32, (rows, 8), 0)
    c = lax.broadcasted_iota(I32, (rows, 8), 1)
    pick = jnp.where(c == (n % SEQ_PAIR) * 4 + (r % 16) // 4, 1.0, 0.0).astype(BF16)
    sel = _dot(pick, sel_ref[...].astype(BF16))

    cb = cb_ref[...]
    bias = bias_ref[...]
    _wait_pages(kcache, kbuf.at[slot], sem.at[0, slot], n_pages, page)
    mx = jnp.full((rows, LANES), NEG, F32)
    for ch in range(n_chunks):
        kc = kbuf[slot, :, ch * KEY_CHUNK:(ch + 1) * KEY_CHUNK].astype(BF16)
        lg = _dot(q32, kc) * ATTN_SCALE + cb + sel[:, ch * KEY_CHUNK:(ch + 1) * KEY_CHUNK]
        if ch == n_chunks - 1:
            near = jnp.concatenate([jnp.zeros((rows, KEY_CHUNK - LANES), F32), bias[:, :LANES] - cb], axis=1)
            lg = lg + near
        lg_ref[ch] = lg
        for t in range(KEY_CHUNK // LANES):
            mx = jnp.maximum(mx, lg[:, t * LANES:(t + 1) * LANES])
    new_ref[...] = jnp.zeros(new_ref.shape, F32)
    new_ref[0:4, :] = kn_ref[0]
    lg_new = _dot_nt(q32, new_ref[...].astype(BF16)) * ATTN_SCALE + bias[:, LANES:] + sel[:, past:]
    m = jnp.max(jnp.maximum(mx, lg_new), axis=1, keepdims=True)

    _wait_pages(vcache, vbuf.at[slot], sem.at[1, slot], n_pages, page)
    new_ref[0:4, :] = vn_ref[0]
    p_new = jnp.exp(lg_new - m)
    out = _dot(p_new.astype(BF16), new_ref[...].astype(BF16))
    lsum = p_new
    for ch in range(n_chunks):
        pr = jnp.exp(lg_ref[ch] - m)
        out = out + _dot_nt(pr.astype(BF16), vbuf[slot, :, ch * KEY_CHUNK:(ch + 1) * KEY_CHUNK].astype(BF16))
        for t in range(KEY_CHUNK // LANES):
            lsum = lsum + pr[:, t * LANES:(t + 1) * LANES]
    out = out / jnp.sum(lsum, axis=1, keepdims=True)
    o_ref[...] = jnp.where(lo_half, out[:rows // 2], out[rows // 2:]).astype(BF16)


def _dsa_sample_attn_call(page_table, qa16, sel, bias32, cb32, ka_new, va_new, cache_k_a, cache_v_a):
    n, n_pages = page_table.shape
    page = cache_k_a.shape[-1]
    past = n_pages * page
    width = past + LANES
    rows = N_HEADS_A * 4
    kv_w = N_KV_A * HEAD_DIM
    return pl.pallas_call(
        functools.partial(_dsa_sample_attn_kernel, n_pages=n_pages, page=page),
        grid_spec=pltpu.PrefetchScalarGridSpec(
            num_scalar_prefetch=1,
            grid=(n,),
            in_specs=[pl.BlockSpec((rows // 2, LANES), lambda s, pt: (s, 0)),
                      pl.BlockSpec((SEQ_PAIR * 4, width), lambda s, pt: (s // SEQ_PAIR, 0)),
                      pl.BlockSpec(bias32.shape, lambda s, pt: (0, 0)),
                      pl.BlockSpec(cb32.shape, lambda s, pt: (0, 0)),
                      pl.BlockSpec((1, 4, kv_w), lambda s, pt: (s, 0, 0)),
                      pl.BlockSpec((1, 4, kv_w), lambda s, pt: (s, 0, 0)),
                      pl.BlockSpec(memory_space=pl.ANY),
                      pl.BlockSpec(memory_space=pl.ANY)],
            out_specs=pl.BlockSpec((rows // 2, LANES), lambda s, pt: (s, 0)),
            scratch_shapes=[pltpu.VMEM((2, kv_w, past), F32),
                            pltpu.VMEM((2, kv_w, past), F32),
                            pltpu.SemaphoreType.DMA((2, 2)),
                            pltpu.VMEM((past // KEY_CHUNK, rows, KEY_CHUNK), F32),
                            pltpu.VMEM((LANES, kv_w), F32)]),
        out_shape=jax.ShapeDtypeStruct((n * rows // 2, LANES), BF16),
        compiler_params=pltpu.CompilerParams(dimension_semantics=("arbitrary",), vmem_limit_bytes=VMEM_LIMIT),
        name="dsa_sample_attn",
    )(page_table, qa16, sel, bias32, cb32, ka_new, va_new, cache_k_a, cache_v_a)


def _sb_sample_kernel(pt_ref, qb_ref, kn_ref, vn_ref, kcache, vcache, to_ref, o_ref,
                      kbuf, vbuf, sem, qbd_ref, acc_ref, oacc_ref, new_ref, *, n_pages, page):
    n = pl.program_id(0)
    c = pl.program_id(1)
    n_chunks = n_pages * page // KEY_CHUNK
    step = n * n_chunks + c
    slot = step % 2
    pages_per_chunk = KEY_CHUNK // page
    rows = N_HEADS_B * 4
    width = N_HEADS_B * HEAD_DIM
    tiles = KEY_CHUNK // LANES

    def start(st, sl):
        seq = st // n_chunks
        first = n_pages - (st % n_chunks + 1) * pages_per_chunk
        _start_pages(kcache, kbuf.at[sl], sem.at[0, sl], pt_ref, seq, first, pages_per_chunk, page)
        _start_pages(vcache, vbuf.at[sl], sem.at[1, sl], pt_ref, seq, first, pages_per_chunk, page)

    @pl.when(step == 0)
    def _():
        start(step, slot)

    @pl.when(step + 1 < pl.num_programs(0) * n_chunks)
    def _():
        start(step + 1, 1 - slot)

    @pl.when(c == 0)
    def _():
        q = qb_ref[0].astype(F32)
        q32 = jnp.concatenate([jnp.broadcast_to(q[t:t + 1, :], (N_HEADS_B, width)) for t in range(4)], axis=0)
        r = lax.broadcasted_iota(I32, (rows, width), 0)
        l = lax.broadcasted_iota(I32, (rows, width), 1)
        qbd_ref[...] = jnp.where(l // HEAD_DIM == r % N_HEADS_B, q32, 0.0).astype(BF16)
        row = lax.broadcasted_iota(I32, (rows, LANES), 0)
        lane = lax.broadcasted_iota(I32, (rows, LANES), 1)
        mask = jnp.logical_and(lane < row // N_HEADS_B, lane < 4)
        new_ref[...] = jnp.zeros(new_ref.shape, F32)
        new_ref[0:4, :] = kn_ref[0]
        z = _dot_nt(qbd_ref[...], new_ref[...].astype(BF16)) * ATTN_SCALE
        sp = _softplus(z)
        hi, lo = _split_bf16(jnp.where(mask, -sp, 0.0))
        cs = _dot(hi, to_ref[...]) + _dot(lo, to_ref[...])
        a = jnp.where(mask, jnp.exp(z - sp + cs[:, :LANES]), 0.0)
        new_ref[0:4, :] = vn_ref[0]
        oacc_ref[...] = _dot(a.astype(BF16), new_ref[...].astype(BF16))
        acc_ref[...] = cs[:, LANES:]

    _wait_pages(kcache, kbuf.at[slot], sem.at[0, slot], pages_per_chunk, page)
    z = _dot(qbd_ref[...], kbuf[slot].astype(BF16)) * ATTN_SCALE
    sp = _softplus(z)
    stack = jnp.concatenate([-sp[:, t * LANES:(t + 1) * LANES] for t in range(tiles)], axis=0)
    hi, lo = _split_bf16(stack)
    cs = _dot(hi, to_ref[...]) + _dot(lo, to_ref[...])
    run = acc_ref[...]
    parts = [None] * tiles
    for t in reversed(range(tiles)):
        after = cs[t * rows:(t + 1) * rows, :LANES] + run
        run = run + cs[t * rows:(t + 1) * rows, LANES:]
        sl = slice(t * LANES, (t + 1) * LANES)
        parts[t] = jnp.exp(z[:, sl] - sp[:, sl] + after)
    acc_ref[...] = run
    a = jnp.concatenate(parts, axis=1).astype(BF16)
    _wait_pages(vcache, vbuf.at[slot], sem.at[1, slot], pages_per_chunk, page)
    oacc_ref[...] += _dot_nt(a, vbuf[slot].astype(BF16))

    @pl.when(c == n_chunks - 1)
    def _():
        r = lax.broadcasted_iota(I32, (rows, width), 0)
        l = lax.broadcasted_iota(I32, (rows, width), 1)
        diag = jnp.where(l // HEAD_DIM == r % N_HEADS_B, oacc_ref[...], 0.0)
        o_ref[0] = jnp.sum(diag.reshape(4, N_HEADS_B, width), axis=1)


def _sb_sample_call(page_table, qb4, kb_new, vb_new, cache_k_b, cache_v_b, to):
    n, n_pages = page_table.shape
    page = cache_k_b.shape[-1]
    width = N_HEADS_B * HEAD_DIM
    rows = N_HEADS_B * 4
    tok = pl.BlockSpec((1, 4, width), lambda s, c, pt: (s, 0, 0))
    return pl.pallas_call(
        functools.partial(_sb_sample_kernel, n_pages=n_pages, page=page),
        grid_spec=pltpu.PrefetchScalarGridSpec(
            num_scalar_prefetch=1,
            grid=(n, n_pages * page // KEY_CHUNK),
            in_specs=[tok, tok, tok,
                      pl.BlockSpec(memory_space=pl.ANY), pl.BlockSpec(memory_space=pl.ANY),
                      pl.BlockSpec(to.shape, lambda s, c, pt: (0, 0))],
            out_specs=tok,
            scratch_shapes=[pltpu.VMEM((2, width, KEY_CHUNK), F32),
                            pltpu.VMEM((2, width, KEY_CHUNK), F32),
                            pltpu.SemaphoreType.DMA((2, 2)),
                            pltpu.VMEM((rows, width), BF16),
                            pltpu.VMEM((rows, LANES), F32),
                            pltpu.VMEM((rows, width), F32),
                            pltpu.VMEM((LANES, width), F32)]),
        out_shape=jax.ShapeDtypeStruct((n, 4, width), F32),
        compiler_params=pltpu.CompilerParams(dimension_semantics=("arbitrary", "arbitrary"),
                                             vmem_limit_bytes=VMEM_LIMIT),
        name="sb_sample",
    )(page_table, qb4, kb_new, vb_new, cache_k_b, cache_v_b, to)


def _prep_weights(w_in, w_o_a, w_o_b, w_out, w_up1, w_down1, w_up2, w_down2):
    w = w_in[0]
    d_model = w.shape[0]
    width_a = N_HEADS_A * HEAD_DIM
    kv_a = N_KV_A * HEAD_DIM
    width_i = N_IDX_HEADS * IDX_DIM
    width_b = N_HEADS_B * HEAD_DIM
    sizes = (width_a, kv_a, kv_a, width_i, IDX_DIM, N_IDX_HEADS, width_b, width_b, width_b, d_model, d_model)
    offs = np.cumsum((0,) + sizes)
    q_a, k_a, v_a, q_i, k_i, w_i, q_b, k_b, v_b, g_a, g_b = [w[:, int(offs[n]):int(offs[n + 1])] for n in range(len(sizes))]
    perm = np.concatenate([np.arange(HEAD_DIM) + (g * GROUP_A + j) * HEAD_DIM
                           for j in range(GROUP_A) for g in range(N_KV_A)])
    pad = jnp.zeros((d_model, LANES - IDX_DIM - N_IDX_HEADS), w.dtype)
    proj = [q_a[:, perm], k_a, v_a, q_i, jnp.concatenate([k_i, w_i, pad], axis=1),
            jnp.concatenate([k_i, k_i], axis=1), q_b, k_b, v_b, g_a, g_b]
    bf = lambda a: a.astype(BF16)
    return dict(
        proj=[bf(p) for p in proj],
        woa=bf(w_o_a[0][perm, :]), wob=bf(w_o_b[0]), wout=bf(w_out[0]),
        wup1=bf(w_up1[0]), wdn1=bf(w_down1[0]), wup2=bf(w_up2[0]), wdn2=bf(w_down2[0]),
    )


def _prompt_tables(rel_bias):
    r = np.arange(Q_TILE)[:, None]
    c = np.arange(2 * LANES)[None, :]
    bias_tile = _bias_call(rel_bias, _t5_bucket_np(LANES + r - c))
    return bias_tile


def _prompt_path(xp, wts, rel_bias, g_ffn1, g_mix, g_ffn2, g_final, batch, seq):
    x1 = _ffn_call(xp, g_ffn1, wts["wup1"], wts["wdn1"])
    pr = _proj_call(x1, g_mix, wts["proj"])
    bias_tile = _prompt_tables(rel_bias)
    oa = _dsa_prompt_call(rel_bias, pr, bias_tile, _tri_ones(True), batch, seq)
    ob = _sb_prompt_call(pr, _tri_ones(False), batch, seq)
    y = _post_call(x1, oa, ob, pr["sa"], pr["sb"], wts["woa"], wts["wob"], wts["wout"],
                   g_ffn2, wts["wup2"], wts["wdn2"], g_final)
    return y, pr


def _sample_tables(rel_bias):
    t = np.arange(4)[:, None]
    c = np.arange(LANES)[None, :]
    idx = np.concatenate([_t5_bucket_np(LANES + t - c), _t5_bucket_np(t - c)], axis=1)
    bias = _bias_call(rel_bias, idx)
    bias32 = bias.reshape(N_KV_A, GROUP_A, 4, 2 * LANES).transpose(0, 2, 1, 3).reshape(N_HEADS_A * 4, 2 * LANES)
    far = rel_bias[N_BUCKETS - 1].reshape(N_KV_A, 1, GROUP_A, 1)
    cb32 = jnp.broadcast_to(far, (N_KV_A, 4, GROUP_A, 1)).reshape(N_HEADS_A * 4, 1)
    return bias32, cb32


def _sample_path(xs, wts, rel_bias, caches, page_table, g_ffn1, g_mix, g_ffn2, g_final, n_seq, n_tok):
    assert n_tok == 4 and n_seq % SEQ_PAIR == 0
    n_pages = page_table.shape[1]
    page = caches[2].shape[2]

    def key_minor(c):
        nd = c.ndim
        c = jnp.transpose(c, (0, 1) + tuple(range(3, nd)) + (2,))
        return c.reshape(c.shape[:2] + (-1, page))
    cache_k_a, cache_v_a, cache_k_idx, cache_k_b, cache_v_b = [key_minor(c) for c in caches]
    assert (n_pages * page) % KEY_CHUNK == 0 and KEY_CHUNK % page == 0
    x1 = _ffn_call(xs, g_ffn1, wts["wup1"], wts["wdn1"])
    pr = _proj_call(x1, g_mix, wts["proj"])
    topk = min(TOPK_MAX, (n_pages * page + n_tok) // 4)
    half = n_seq // SEQ_PAIR
    qi64 = pr["qi"].reshape(half, SEQ_PAIR, 4, N_IDX_HEADS, IDX_DIM).transpose(0, 3, 1, 2, 4)
    qi64 = qi64.reshape(half * N_IDX_HEADS * SEQ_PAIR * 4, IDX_DIM)
    wcol = pr["kw"][:, IDX_DIM:IDX_DIM + N_IDX_HEADS].reshape(half, SEQ_PAIR, 4, N_IDX_HEADS).transpose(0, 3, 1, 2)
    wcol = wcol.reshape(half * N_IDX_HEADS * SEQ_PAIR * 4, 1)
    ki_new = pr["kw"][:, :IDX_DIM].reshape(n_seq, 4, IDX_DIM)
    sel = _dsa_sample_select_call(page_table, qi64, wcol, ki_new, cache_k_idx, _tri_ones(True), topk)
    bias32, cb32 = _sample_tables(rel_bias)
    oa = _dsa_sample_attn_call(page_table, pr["qa"].reshape(n_seq * 16, LANES), sel, bias32, cb32,
                               pr["ka"].reshape(n_seq, 4, -1), pr["va"].reshape(n_seq, 4, -1), cache_k_a, cache_v_a)
    ob = _sb_sample_call(page_table, pr["qb"].reshape(n_seq, 4, -1), pr["kb"].reshape(n_seq, 4, -1),
                         pr["vb"].reshape(n_seq, 4, -1), cache_k_b, cache_v_b, _tri_ones(False))
    oa = oa.reshape(n_seq * 4, -1)
    ob = ob.reshape(n_seq * 4, -1).astype(BF16)
    y = _post_call(x1, oa, ob, pr["sa"], pr["sb"], wts["woa"], wts["wob"], wts["wout"],
                   g_ffn2, wts["wup2"], wts["wdn2"], g_final)
    return y, pr


def _rows(pr, lead):
    depth = (1,)
    return (pr["ka"].reshape(depth + lead + (N_KV_A, HEAD_DIM)),
            pr["va"].reshape(depth + lead + (N_KV_A, HEAD_DIM)),
            pr["kw"][:, :IDX_DIM].reshape(depth + lead + (IDX_DIM,)),
            pr["kb"].reshape(depth + lead + (N_HEADS_B, HEAD_DIM)),
            pr["vb"].reshape(depth + lead + (N_HEADS_B, HEAD_DIM)))


def kernel(x_prompt, x_sample, cache_k_a, cache_v_a, cache_k_idx, cache_k_b, cache_v_b, page_table,
           w_in, w_o_a, w_o_b, w_out, rel_bias, g_ffn1, w_up1, w_down1, g_mix, g_ffn2, w_up2, w_down2, g_final):
    assert w_in.shape[0] == 1, "single-layer step"
    batch, seq, d_model = x_prompt.shape
    n_seq, n_tok, _ = x_sample.shape
    assert seq % Q_TILE == 0
    wts = _prep_weights(w_in, w_o_a, w_o_b, w_out, w_up1, w_down1, w_up2, w_down2)
    gf = g_final.reshape(1, d_model)
    yp, prp = _prompt_path(x_prompt.reshape(batch * seq, d_model), wts, rel_bias,
                           g_ffn1, g_mix, g_ffn2, gf, batch, seq)
    ys, prs = _sample_path(x_sample.reshape(n_seq * n_tok, d_model), wts, rel_bias,
                           (cache_k_a, cache_v_a, cache_k_idx, cache_k_b, cache_v_b), page_table,
                           g_ffn1, g_mix, g_ffn2, gf, n_seq, n_tok)
    return ((yp.reshape(batch, seq, d_model), ys.reshape(n_seq, n_tok, d_model))
            + _rows(prp, (batch, seq)) + _rows(prs, (n_seq, n_tok)))
```

```python
import functools
import math

import jax
import jax.numpy as jnp
import numpy as np
from jax import lax
from jax.experimental import pallas as pl
from jax.experimental.pallas import tpu as pltpu

F32 = jnp.float32
BF16 = jnp.bfloat16
I32 = jnp.int32

HEAD_DIM = 64
IDX_DIM = 64
N_HEADS_A = 8
N_KV_A = 2
GROUP_A = N_HEADS_A // N_KV_A
N_IDX_HEADS = 8
N_HEADS_B = 8
TOPK_MAX = 256
N_BUCKETS = 32
MAX_DISTANCE = 128
EPS = 1e-6
ATTN_SCALE = HEAD_DIM ** -0.5
IDX_SCALE = IDX_DIM ** -0.5
IDX_HEAD_SCALE = N_IDX_HEADS ** -0.5

LANES = 128
Q_TILE = 128
NEG = -1e30
INT_MIN = -2 ** 31
VMEM_LIMIT = 56 * 1024 * 1024
FFN_CHUNK = 256
TOKEN_TILE = 512
SUPER = 512
SUB = SUPER // LANES
TL_ROWS = LANES + 16


def _pick_tile(n, pref):
    t = min(n, pref)
    while n % t or t % 8:
        t -= 1
    return t


def _const_spec(shape):
    nd = len(shape)
    return pl.BlockSpec(shape, lambda *_: (0,) * nd, pipeline_mode=pl.Buffered(1))


def _dot(a, b):
    return jnp.dot(a, b, preferred_element_type=F32)


def _dot_nt(a, b):
    return lax.dot_general(a, b, (((1,), (1,)), ((), ())), preferred_element_type=F32)


def _rms(x, g):
    r = lax.rsqrt(jnp.mean(x * x, axis=-1, keepdims=True) + EPS)
    return (x * r) * g


def _softplus(z):
    return jnp.maximum(z, 0.0) + jnp.log(1.0 + jnp.exp(-jnp.abs(z)))


def _split_bf16(x):
    hi = x.astype(BF16)
    lo = (x - hi.astype(F32)).astype(BF16)
    return hi, lo


def _sortable(x):
    bits = lax.bitcast_convert_type(x + 0.0, I32)
    return bits ^ ((bits >> 31) & jnp.int32(0x7FFFFFFF))


def _swiglu(x, g_ref, wup_ref, wdn_ref, act_ref):
    d_ff = wdn_ref.shape[0]
    h = _rms(x, g_ref[...]).astype(BF16)
    for c in range(d_ff // FFN_CHUNK):
        lo = c * FFN_CHUNK
        gate = _dot(h, wup_ref[:, lo:lo + FFN_CHUNK])
        up = _dot(h, wup_ref[:, d_ff + lo:d_ff + lo + FFN_CHUNK])
        act_ref[:, lo:lo + FFN_CHUNK] = (gate * jax.nn.sigmoid(gate) * up).astype(BF16)
    return _dot(act_ref[...], wdn_ref[...])


def _ffn_kernel(x_ref, g_ref, wup_ref, wdn_ref, o_ref, act_ref):
    x = x_ref[...]
    o_ref[...] = x + 0.5 * _swiglu(x, g_ref, wup_ref, wdn_ref, act_ref)


def _ffn_call(x, g, wup, wdn):
    n, d = x.shape
    d_ff = wdn.shape[0]
    tm = _pick_tile(n, TOKEN_TILE)
    row = lambda w: pl.BlockSpec((tm, w), lambda i: (i, 0))
    return pl.pallas_call(
        _ffn_kernel,
        grid=(n // tm,),
        in_specs=[row(d), _const_spec(g.shape), _const_spec(wup.shape), _const_spec(wdn.shape)],
        out_specs=row(d),
        out_shape=jax.ShapeDtypeStruct((n, d), F32),
        scratch_shapes=[pltpu.VMEM((tm, d_ff), BF16)],
        compiler_params=pltpu.CompilerParams(dimension_semantics=("arbitrary",), vmem_limit_bytes=VMEM_LIMIT),
        name="ffn_pre",
    )(x, g, wup, wdn)


def _proj_kernel(x_ref, g_ref, wqa, wka, wva, wqi, wkw, wkk, wqb, wkb, wvb, wga, wgb,
                 qa, ka, va, kab, vab, qi, kw, kk, qb, kb, vb, kbb, vbb, sa, sb):
    h = _rms(x_ref[...], g_ref[...]).astype(BF16)
    qa[...] = _dot(h, wqa[...]).astype(BF16)
    k = _dot(h, wka[...])
    ka[...] = k
    kab[...] = k.astype(BF16)
    v = _dot(h, wva[...])
    va[...] = v
    vab[...] = v.astype(BF16)
    qi[...] = _dot(h, wqi[...]).astype(BF16)
    kw[...] = _dot(h, wkw[...])
    kk[...] = _dot(h, wkk[...]).astype(BF16)
    qb[...] = _dot(h, wqb[...]).astype(BF16)
    k = _dot(h, wkb[...])
    kb[...] = k
    kbb[...] = k.astype(BF16)
    v = _dot(h, wvb[...])
    vb[...] = v
    vbb[...] = v.astype(BF16)
    sa[...] = jax.nn.sigmoid(_dot(h, wga[...]))
    sb[...] = jax.nn.sigmoid(_dot(h, wgb[...]))


_PROJ_OUT = (("qa", 512, BF16), ("ka", 128, F32), ("va", 128, F32), ("kab", 128, BF16), ("vab", 128, BF16),
             ("qi", 512, BF16), ("kw", 128, F32), ("kk", 128, BF16), ("qb", 512, BF16),
             ("kb", 512, F32), ("vb", 512, F32), ("kbb", 512, BF16), ("vbb", 512, BF16),
             ("sa", 1024, F32), ("sb", 1024, F32))


def _proj_call(x, g, weights):
    n, d = x.shape
    tm = _pick_tile(n, TOKEN_TILE)
    row = lambda w: pl.BlockSpec((tm, w), lambda i: (i, 0))
    outs = pl.pallas_call(
        _proj_kernel,
        grid=(n // tm,),
        in_specs=[row(d), _const_spec(g.shape)] + [_const_spec(w.shape) for w in weights],
        out_specs=[row(w) for _, w, _ in _PROJ_OUT],
        out_shape=[jax.ShapeDtypeStruct((n, w), dt) for _, w, dt in _PROJ_OUT],
        compiler_params=pltpu.CompilerParams(dimension_semantics=("arbitrary",), vmem_limit_bytes=VMEM_LIMIT),
        name="proj",
    )(x, g, *weights)
    return dict(zip([nm for nm, _, _ in _PROJ_OUT], outs))


_PROMPT_ROW_OUT = (("qa", 512, BF16), ("kab", 128, BF16), ("qi", 512, BF16), ("kk", 128, BF16),
                   ("qb", 512, BF16), ("kbb", 512, BF16), ("sa", 1024, F32), ("sb", 1024, F32))
_PROMPT_COL_OUT = (("kat", 0, 128), ("vat", 128, 128), ("kwt", 256, 128), ("kbt", 384, 512), ("vbt", 896, 512))


def _proj_prompt_kernel(x_ref, g_ref, wqa, wka, wqi, wkk, wqb, wkb, wga, wgb, wt_ref,
                        qa, kab, qi, kk, qb, kbb, sa, sb, kat, vat, kwt, kbt, vbt, vatb, vbtb):
    h = _rms(x_ref[...], g_ref[...]).astype(BF16)
    qa[...] = _dot(h, wqa[...]).astype(BF16)
    kab[...] = _dot(h, wka[...]).astype(BF16)
    qi[...] = _dot(h, wqi[...]).astype(BF16)
    kk[...] = _dot(h, wkk[...]).astype(BF16)
    qb[...] = _dot(h, wqb[...]).astype(BF16)
    kbb[...] = _dot(h, wkb[...]).astype(BF16)
    sa[...] = jax.nn.sigmoid(_dot(h, wga[...]))
    sb[...] = jax.nn.sigmoid(_dot(h, wgb[...]))
    t = _dot_nt(wt_ref[...], h)
    for ref, (_, r0, nr) in zip((kat, vat, kwt, kbt, vbt), _PROMPT_COL_OUT):
        ref[0] = t[r0:r0 + nr]
    vatb[0, 0] = t[128:256].astype(BF16)
    vbtb[0, 0] = t[896:1408].astype(BF16)


def _proj_prompt_call(x, g, weights, wt_all, batch, seq):
    n, d = x.shape
    tm = SUPER
    per_b = seq // tm
    row = lambda w: pl.BlockSpec((tm, w), lambda i: (i, 0))
    col = lambda r: pl.BlockSpec((1, r, tm), lambda i: (i // per_b, 0, i % per_b))
    blk = lambda r: pl.BlockSpec((1, 1, r, tm), lambda i: (i // per_b, i % per_b, 0, 0))
    out_specs = ([row(w) for _, w, _ in _PROMPT_ROW_OUT] + [col(nr) for _, _, nr in _PROMPT_COL_OUT]
                 + [blk(128), blk(512)])
    out_shape = ([jax.ShapeDtypeStruct((n, w), dt) for _, w, dt in _PROMPT_ROW_OUT]
                 + [jax.ShapeDtypeStruct((batch, nr, seq), F32) for _, _, nr in _PROMPT_COL_OUT]
                 + [jax.ShapeDtypeStruct((batch, per_b, 128, tm), BF16),
                    jax.ShapeDtypeStruct((batch, per_b, 512, tm), BF16)])
    outs = pl.pallas_call(
        _proj_prompt_kernel,
        grid=(n // tm,),
        in_specs=[row(d), _const_spec(g.shape)] + [_const_spec(w.shape) for w in weights] + [_const_spec(wt_all.shape)],
        out_specs=out_specs,
        out_shape=out_shape,
        compiler_params=pltpu.CompilerParams(dimension_semantics=("arbitrary",), vmem_limit_bytes=VMEM_LIMIT),
        name="proj_prompt",
    )(x, g, *weights, wt_all)
    names = [nm for nm, _, _ in _PROMPT_ROW_OUT] + [nm for nm, _, _ in _PROMPT_COL_OUT] + ["vatb", "vbtb"]
    return dict(zip(names, outs))


def _post_kernel(x_ref, oa_ref, ob_ref, sa_ref, sb_ref, woa, wob, wout, g2_ref, wup_ref, wdn_ref, gf_ref,
                 y_ref, act_ref):
    mix = sa_ref[...] * _dot(oa_ref[...], woa[...]) + sb_ref[...] * _dot(ob_ref[...], wob[...])
    x2 = x_ref[...] + _dot(mix.astype(BF16), wout[...])
    x3 = x2 + 0.5 * _swiglu(x2, g2_ref, wup_ref, wdn_ref, act_ref)
    y_ref[...] = _rms(x3, gf_ref[...])


def _post_call(x, oa, ob, sa, sb, woa, wob, wout, g2, wup, wdn, gf):
    n, d = x.shape
    d_ff = wdn.shape[0]
    tm = _pick_tile(n, TOKEN_TILE)
    row = lambda w: pl.BlockSpec((tm, w), lambda i: (i, 0))
    consts = (woa, wob, wout, g2, wup, wdn, gf)
    return pl.pallas_call(
        _post_kernel,
        grid=(n // tm,),
        in_specs=[row(d), row(oa.shape[1]), row(ob.shape[1]), row(d), row(d)] + [_const_spec(c.shape) for c in consts],
        out_specs=row(d),
        out_shape=jax.ShapeDtypeStruct((n, d), F32),
        scratch_shapes=[pltpu.VMEM((tm, d_ff), BF16)],
        compiler_params=pltpu.CompilerParams(dimension_semantics=("arbitrary",), vmem_limit_bytes=VMEM_LIMIT),
        name="post",
    )(x, oa, ob, sa, sb, *consts)


def _t5_bucket_np(dist):
    max_exact = N_BUCKETS // 2
    d = np.maximum(dist, 0)
    ratio = np.maximum(d, 1).astype(np.float32) / np.float32(max_exact)
    large = max_exact + (np.log(ratio) / np.float32(math.log(MAX_DISTANCE / max_exact))
                         * np.float32(N_BUCKETS - max_exact)).astype(np.int32)
    large = np.minimum(large, N_BUCKETS - 1)
    return np.where(d < max_exact, d, large).astype(np.int32)


def _bias_kernel(rel_ref, idx_ref, out_ref):
    idx = idx_ref[...]
    for h in range(out_ref.shape[0]):
        acc = jnp.zeros(idx.shape, F32)
        for b in range(N_BUCKETS):
            acc = jnp.where(idx == b, rel_ref[b, h], acc)
        out_ref[h] = acc


def _bias_call(rel_bias, idx):
    return pl.pallas_call(
        _bias_kernel,
        in_specs=[pl.BlockSpec(memory_space=pltpu.SMEM), pl.BlockSpec(memory_space=pltpu.VMEM)],
        out_specs=pl.BlockSpec(memory_space=pltpu.VMEM),
        out_shape=jax.ShapeDtypeStruct((rel_bias.shape[1],) + idx.shape, F32),
        name="rel_bias_table",
    )(rel_bias, jnp.asarray(idx))


def _half_masks():
    lane = lax.broadcasted_iota(I32, (Q_TILE, LANES), 1)
    return lane < HEAD_DIM


def _store_masked_pairs(src_ref, dst_ref, n_pairs, scale=1.0):
    lo_half = _half_masks()
    for p in range(n_pairs):
        pair = src_ref[:, p * LANES:(p + 1) * LANES].astype(F32) * scale
        dst_ref[p, 0:Q_TILE, :] = jnp.where(lo_half, pair, 0.0).astype(BF16)
        dst_ref[p, Q_TILE:2 * Q_TILE, :] = jnp.where(lo_half, 0.0, pair).astype(BF16)


def _tri_ones(strict_upper):
    r = np.arange(LANES)
    if strict_upper:
        tri = (r[:, None] < r[None, :])
    else:
        tri = (r[:, None] > r[None, :])
    return jnp.asarray(np.concatenate([tri, np.ones((LANES, LANES), bool)], axis=1), dtype=BF16)


def _dsa_prompt_kernel(qi_ref, kwt_ref, kk_ref, qa_ref, ka_ref, vat_ref, bias_ref, lt_ref, o_ref,
                       qim_ref, qam_ref, key_ref, sel_ref, lg_ref, mx_ref, ls_ref, out_ref, *, topk):
    i = pl.program_id(1)
    ntile = i + 1
    n_super = i // SUB + 1
    n_pairs = N_HEADS_A // 2
    row = lax.broadcasted_iota(I32, (LANES, LANES), 0)
    lane = lax.broadcasted_iota(I32, (LANES, LANES), 1)
    key_minus_query = row - lane

    _store_masked_pairs(qi_ref, qim_ref, N_IDX_HEADS // 2, IDX_SCALE)
    _store_masked_pairs(qa_ref, qam_ref, n_pairs, ATTN_SCALE)
    wt = kwt_ref[0, IDX_DIM:IDX_DIM + N_IDX_HEADS, :] * IDX_HEAD_SCALE
    mx_ref[...] = jnp.full(mx_ref.shape, NEG, F32)
    ls_ref[...] = jnp.zeros(ls_ref.shape, F32)
    out_ref[...] = jnp.zeros(out_ref.shape, F32)

    def super_rows(jt):
        return pl.ds(pl.multiple_of(jt * SUPER, SUPER), SUPER)

    def valid_tile(j):
        return key_minus_query <= (i - j) * LANES

    def score_super(jt, c):
        kt = kk_ref[super_rows(jt), :]
        acc = jnp.zeros((SUPER, LANES), F32)
        for p in range(N_IDX_HEADS // 2):
            s = _dot_nt(kt, qim_ref[p])
            acc = acc + wt[2 * p:2 * p + 1] * jnp.maximum(s[:, :LANES], 0.0)
            acc = acc + wt[2 * p + 1:2 * p + 2] * jnp.maximum(s[:, LANES:], 0.0)
        keys = _sortable(acc)
        for k in range(SUB):
            j = jt * SUB + k
            key_ref[j] = jnp.where(valid_tile(j), keys[k * LANES:(k + 1) * LANES], INT_MIN)
        return c
    lax.fori_loop(0, n_super, score_super, 0)

    def count_tiles(pred):
        def body(j, cnt):
            return cnt + jnp.where(pred(key_ref[j]), 1, 0)
        cnt = lax.fori_loop(0, ntile, body, jnp.zeros((LANES, LANES), I32))
        return jnp.sum(cnt, axis=0, keepdims=True)

    def bit_step(it, u):
        bit = lax.shift_left(jnp.int32(1), 31 - it)
        cand = (u | bit) ^ INT_MIN
        tot = count_tiles(lambda k: k >= cand)
        return jnp.where(tot >= topk, u | bit, u)
    thr = lax.fori_loop(0, 32, bit_step, jnp.zeros((1, LANES), I32)) ^ INT_MIN
    need = (topk - count_tiles(lambda k: k > thr)).astype(F32)

    over = jnp.logical_and(count_tiles(lambda k: k >= thr) > topk, thr > INT_MIN)
    ties = jnp.sum(jnp.where(over, 1, 0)) > 0

    @pl.when(ties)
    def _():
        def select_tile(j, carry):
            key = key_ref[j]
            eq = key == thr
            cs = _dot(lt_ref[...], jnp.where(eq, 1.0, 0.0).astype(BF16))
            take = jnp.logical_or(key > thr, jnp.logical_and(eq, cs[:LANES] + carry < need))
            sel_ref[j] = jnp.where(jnp.logical_and(take, valid_tile(j)), 0.0, NEG)
            return carry + cs[LANES:LANES + 1]
        lax.fori_loop(0, n_super * SUB, select_tile, jnp.zeros((1, LANES), F32))

    @pl.when(jnp.logical_not(ties))
    def _():
        def select_tile(j, c):
            take = jnp.logical_and(key_ref[j] >= thr, valid_tile(j))
            sel_ref[j] = jnp.where(take, 0.0, NEG)
            return c
        lax.fori_loop(0, n_super * SUB, select_tile, 0)

    def logit_super(jt, c):
        kt = ka_ref[super_rows(jt), :]
        for p in range(n_pairs):
            s = _dot_nt(kt, qam_ref[p])
            mx = mx_ref[p]
            for k in range(SUB):
                j = jt * SUB + k
                rel = jnp.clip(i - j, 0, 2)
                sel = sel_ref[j]
                sk = s[k * LANES:(k + 1) * LANES]
                lg0 = sk[:, :LANES] + bias_ref[p, rel] + sel
                lg1 = sk[:, LANES:] + bias_ref[GROUP_A + p, rel] + sel
                lg_ref[p, j, :, 0:LANES] = lg0
                lg_ref[p, j, :, LANES:2 * LANES] = lg1
                lg = jnp.concatenate([lg0, lg1], axis=1)
                mx = jnp.maximum(mx, jnp.max(lg.reshape(LANES // 8, 8, 2 * LANES), axis=0))
            mx_ref[p] = mx
        return c
    lax.fori_loop(0, n_super, logit_super, 0)

    for p in range(n_pairs):
        m = jnp.max(mx_ref[p], axis=0, keepdims=True)
        mx_ref[p] = jnp.broadcast_to(m, mx_ref.shape[1:])

    def value_super(jt, c):
        vt = vat_ref[0, jt]
        for p in range(n_pairs):
            m = mx_ref[p][0:1]
            parts = []
            ls = ls_ref[p]
            for k in range(SUB):
                pk = jnp.exp(lg_ref[p, jt * SUB + k] - m)
                ls = ls + jnp.sum(pk.reshape(LANES // 8, 8, 2 * LANES), axis=0)
                parts.append(pk.astype(BF16))
            ls_ref[p] = ls
            out_ref[p] += _dot(vt, jnp.concatenate(parts, axis=0))
        return c
    lax.fori_loop(0, n_super, value_super, 0)

    for p in range(n_pairs):
        denom = jnp.sum(ls_ref[p], axis=0, keepdims=True)
        o_ref[:, p * LANES:(p + 1) * LANES] = _pair_transposed_out(out_ref[p] / denom).astype(BF16)


def _tri_prefix_ones():
    r = np.arange(LANES)
    tri = (r[None, :] < r[:, None])
    return jnp.asarray(np.concatenate([tri, np.ones((TL_ROWS - LANES, LANES), bool)], axis=0), dtype=BF16)


def _dsa_prompt_call(pr, kwt, vat, bias3, batch, seq):
    nq = seq // Q_TILE
    n_pairs = N_HEADS_A // 2
    topk = min(TOPK_MAX, seq // 4)
    lt = _tri_prefix_ones()
    qblk = lambda w: pl.BlockSpec((Q_TILE, w), lambda b, i: (b * nq + i, 0))
    kblk = lambda w: pl.BlockSpec((seq, w), lambda b, i: (b, 0))
    return pl.pallas_call(
        functools.partial(_dsa_prompt_kernel, topk=topk),
        grid=(batch, nq),
        in_specs=[qblk(512), pl.BlockSpec((1, LANES, Q_TILE), lambda b, i: (b, 0, i)), kblk(LANES),
                  qblk(512), kblk(LANES), pl.BlockSpec((1,) + vat.shape[1:], lambda b, i: (b, 0, 0, 0)),
                  _const_spec(bias3.shape), _const_spec(lt.shape)],
        out_specs=qblk(512),
        out_shape=jax.ShapeDtypeStruct((batch * seq, 512), BF16),
        scratch_shapes=[
            pltpu.VMEM((N_IDX_HEADS // 2, 2 * Q_TILE, LANES), BF16),
            pltpu.VMEM((n_pairs, 2 * Q_TILE, LANES), BF16),
            pltpu.VMEM((nq, LANES, Q_TILE), I32),
            pltpu.VMEM((nq, LANES, Q_TILE), F32),
            pltpu.VMEM((n_pairs, nq, LANES, 2 * Q_TILE), F32),
            pltpu.VMEM((n_pairs, 8, 2 * Q_TILE), F32),
            pltpu.VMEM((n_pairs, 8, 2 * Q_TILE), F32),
            pltpu.VMEM((n_pairs, LANES, 2 * Q_TILE), F32),
        ],
        compiler_params=pltpu.CompilerParams(dimension_semantics=("arbitrary", "arbitrary"),
                                             vmem_limit_bytes=VMEM_LIMIT),
        name="dsa_prompt",
    )(pr["qi"], kwt, pr["kk"], pr["qa"], pr["kab"], vat, bias3, lt)


def _tri_lower_ones():
    r = np.arange(LANES)
    tri = (r[None, :] > r[:, None])
    top = np.concatenate([tri, tri], axis=1)
    return jnp.asarray(np.concatenate([top, np.ones((TL_ROWS - LANES, 2 * LANES), bool)], axis=0), dtype=BF16)


def _pair_transposed_out(acc):
    top = lax.broadcasted_iota(I32, (LANES, LANES), 0) < HEAD_DIM
    return jnp.where(top, acc[:, :LANES], acc[:, LANES:]).T


def _sb_prompt_kernel(qb_ref, kb_ref, vbt_ref, tl_ref, o_ref, qm_ref, z_ref, hl_ref, cs_ref, a_ref, run_ref, out_ref):
    i = pl.program_id(1)
    n_pairs = N_HEADS_B // 2
    last = i // SUB

    _store_masked_pairs(qb_ref, qm_ref, n_pairs, ATTN_SCALE)
    run_ref[...] = jnp.zeros(run_ref.shape, F32)
    out_ref[...] = jnp.zeros(out_ref.shape, F32)

    key_row = lax.broadcasted_iota(I32, (SUPER, 2 * LANES), 0)
    q_lane = lax.broadcasted_iota(I32, (SUPER, 2 * LANES), 1) % LANES
    strict = key_row < (i % SUB) * LANES + q_lane

    def super_tile(jt, diag):
        rows = pl.ds(pl.multiple_of(jt * SUPER, SUPER), SUPER)
        for p in range(n_pairs):
            z_ref[p] = _dot_nt(kb_ref[rows, p * LANES:(p + 1) * LANES], qm_ref[p])
        for p in range(n_pairs):
            z = z_ref[p]
            sp = _softplus(z)
            hi, lo = _split_bf16(jnp.where(strict, -sp, 0.0) if diag else -sp)
            for k in range(SUB):
                ks = slice(k * LANES, (k + 1) * LANES)
                hl_ref[p, :, k * 2 * LANES:(k + 1) * 2 * LANES] = jnp.concatenate([hi[ks], lo[ks]], axis=0)
            z_ref[p] = z - sp
        for p in range(n_pairs):
            cs_ref[p] = _dot(tl_ref[...], hl_ref[p])
        for p in range(n_pairs):
            run = run_ref[p]
            for k in reversed(range(SUB)):
                cs = cs_ref[p, :, k * 2 * LANES:(k + 1) * 2 * LANES]
                ks = slice(k * LANES, (k + 1) * LANES)
                a = jnp.exp(z_ref[p, ks, :] + cs[:LANES] + run)
                if diag:
                    a = jnp.where(strict[ks], a, 0.0)
                a_ref[p, ks, :] = a.astype(BF16)
                run = run + cs[LANES:LANES + 1]
            run_ref[p] = run
        for p in range(n_pairs):
            out_ref[p] += _dot(vbt_ref[0, jt, p * LANES:(p + 1) * LANES, :], a_ref[p])

    super_tile(last, True)

    def older(jj, c):
        super_tile(last - 1 - jj, False)
        return c
    lax.fori_loop(0, last, older, 0)

    for p in range(n_pairs):
        o_ref[:, p * LANES:(p + 1) * LANES] = _pair_transposed_out(out_ref[p]).astype(BF16)


def _sb_prompt_call(pr, vbt, tl, batch, seq):
    nq = seq // Q_TILE
    width = N_HEADS_B * HEAD_DIM
    n_pairs = N_HEADS_B // 2
    qblk = pl.BlockSpec((Q_TILE, width), lambda b, i: (b * nq + i, 0))
    kblk = pl.BlockSpec((seq, width), lambda b, i: (b, 0))
    vblk = pl.BlockSpec((1,) + vbt.shape[1:], lambda b, i: (b, 0, 0, 0))
    return pl.pallas_call(
        _sb_prompt_kernel,
        grid=(batch, nq),
        in_specs=[qblk, kblk, vblk, _const_spec(tl.shape)],
        out_specs=qblk,
        out_shape=jax.ShapeDtypeStruct((batch * seq, width), BF16),
        scratch_shapes=[
            pltpu.VMEM((n_pairs, 2 * Q_TILE, LANES), BF16),
            pltpu.VMEM((n_pairs, SUPER, 2 * LANES), F32),
            pltpu.VMEM((n_pairs, 2 * LANES, SUB * 2 * LANES), BF16),
            pltpu.VMEM((n_pairs, TL_ROWS, SUB * 2 * LANES), F32),
            pltpu.VMEM((n_pairs, SUPER, 2 * LANES), BF16),
            pltpu.VMEM((n_pairs, 1, 2 * LANES), F32),
            pltpu.VMEM((n_pairs, LANES, 2 * LANES), F32),
        ],
        compiler_params=pltpu.CompilerParams(dimension_semantics=("arbitrary", "arbitrary"),
                                             vmem_limit_bytes=VMEM_LIMIT),
        name="sb_prompt",
    )(pr["qb"], pr["kbb"], vbt, tl)


SEQ_GROUP = 4
KEY_CHUNK = 1024
SB_CHUNK = 2048


def _page_copy(cache_ref, buf_ref, sem, phys, p, page):
    return pltpu.make_async_copy(cache_ref.at[0, phys], buf_ref.at[:, p * page:(p + 1) * page], sem)


def _start_pages(cache_ref, buf_ref, sem, pt_ref, seq, first_page, n_copy, page):
    for p in range(n_copy):
        _page_copy(cache_ref, buf_ref, sem, pt_ref[seq, first_page + p], p, page).start()


def _wait_pages(cache_ref, buf_ref, sem, n_copy, page):
    for p in range(n_copy):
        _page_copy(cache_ref, buf_ref, sem, 0, p, page).wait()


def _dsa_sample_select_kernel(pt_ref, qi_ref, w_ref, kn_ref, cache_ref, uo_ref, sel_ref,
                              kbuf, sem, key_ref, knew_ref, *, topk, n_pages, page):
    s = pl.program_id(0)
    nstep = pl.num_programs(0)
    slot = s % 2
    past = n_pages * page
    n_tiles = past // LANES + 1
    rows = SEQ_GROUP * 4

    def start(step, sl):
        for g in range(SEQ_GROUP):
            _start_pages(cache_ref, kbuf.at[sl, g], sem.at[sl], pt_ref, step * SEQ_GROUP + g, 0, n_pages, page)

    @pl.when(s == 0)
    def _():
        start(s, slot)

    @pl.when(s + 1 < nstep)
    def _():
        start(s + 1, 1 - slot)

    for g in range(SEQ_GROUP):
        _wait_pages(cache_ref, kbuf.at[slot, g], sem.at[slot], n_pages, page)

    q = qi_ref[...]
    w = w_ref[...] * IDX_HEAD_SCALE
    row = lax.broadcasted_iota(I32, (rows, LANES), 0)
    lane = lax.broadcasted_iota(I32, (rows, LANES), 1)

    def head_sum(sc):
        sc = jnp.maximum(sc * IDX_SCALE, 0.0) * w
        return jnp.sum(sc.reshape(N_IDX_HEADS, rows, sc.shape[-1]), axis=0)

    def own_rows(parts):
        r = lax.broadcasted_iota(I32, parts[0].shape, 0)
        sc = parts[-1]
        for g in reversed(range(SEQ_GROUP - 1)):
            sc = jnp.where(r < 4 * (g + 1), parts[g], sc)
        return sc

    for c in range(past // KEY_CHUNK):
        parts = []
        for g in range(SEQ_GROUP):
            kc = kbuf[slot, g, :, c * KEY_CHUNK:(c + 1) * KEY_CHUNK].astype(BF16)
            parts.append(head_sum(_dot(q, kc)))
        key_ref[:, c * KEY_CHUNK:(c + 1) * KEY_CHUNK] = _sortable(own_rows(parts))

    parts = []
    for g in range(SEQ_GROUP):
        knew_ref[...] = jnp.zeros(knew_ref.shape, F32)
        knew_ref[0:4, :] = kn_ref[g]
        parts.append(head_sum(_dot_nt(q, knew_ref[...].astype(BF16))))
    sc = own_rows(parts)
    valid_new = jnp.logical_and(lane <= row % 4, lane < 4)
    key_ref[:, past:past + LANES] = jnp.where(valid_new, _sortable(sc), INT_MIN)

    key = key_ref[...]

    def bit_step(it, u):
        bit = lax.shift_left(jnp.int32(1), 31 - it)
        cand = (u | bit) ^ INT_MIN
        tot = jnp.sum(jnp.where(key >= cand, 1, 0), axis=1, keepdims=True)
        return jnp.where(tot >= topk, u | bit, u)
    thr = lax.fori_loop(0, 32, bit_step, jnp.zeros((rows, 1), I32)) ^ INT_MIN
    need = (topk - jnp.sum(jnp.where(key > thr, 1, 0), axis=1, keepdims=True)).astype(F32)

    eq = jnp.where(key == thr, 1.0, 0.0)
    stack = jnp.concatenate([eq[:, t * LANES:(t + 1) * LANES] for t in range(n_tiles)], axis=0).astype(BF16)
    cs = _dot(stack, uo_ref[...])
    carry = jnp.zeros((rows, LANES), F32)
    for t in range(n_tiles):
        kt = key[:, t * LANES:(t + 1) * LANES]
        pre = cs[t * rows:(t + 1) * rows, :LANES] + carry
        carry = carry + cs[t * rows:(t + 1) * rows, LANES:]
        take = jnp.logical_or(kt > thr, jnp.logical_and(kt == thr, pre < need))
        if t == n_tiles - 1:
            take = jnp.logical_and(take, valid_new)
        sel_ref[:, t * LANES:(t + 1) * LANES] = jnp.where(take, 0.0, NEG)


def _dsa_sample_select_call(page_table, qi64, wcol, ki_new, cache_k_idx, uo, topk):
    n, n_pages = page_table.shape
    page = cache_k_idx.shape[-1]
    past = n_pages * page
    width = past + LANES
    rows = SEQ_GROUP * 4
    return pl.pallas_call(
        functools.partial(_dsa_sample_select_kernel, topk=topk, n_pages=n_pages, page=page),
        grid_spec=pltpu.PrefetchScalarGridSpec(
            num_scalar_prefetch=1,
            grid=(n // SEQ_GROUP,),
            in_specs=[pl.BlockSpec((N_IDX_HEADS * rows, IDX_DIM), lambda s, pt: (s, 0)),
                      pl.BlockSpec((N_IDX_HEADS * rows, 1), lambda s, pt: (s, 0)),
                      pl.BlockSpec((SEQ_GROUP, 4, IDX_DIM), lambda s, pt: (s, 0, 0)),
                      pl.BlockSpec(memory_space=pl.ANY),
                      pl.BlockSpec(uo.shape, lambda s, pt: (0, 0))],
            out_specs=pl.BlockSpec((rows, width), lambda s, pt: (s, 0)),
            scratch_shapes=[pltpu.VMEM((2, SEQ_GROUP, IDX_DIM, past), F32),
                            pltpu.SemaphoreType.DMA((2,)),
                            pltpu.VMEM((rows, width), I32),
                            pltpu.VMEM((LANES, IDX_DIM), F32)]),
        out_shape=jax.ShapeDtypeStruct((n * 4, width), F32),
        compiler_params=pltpu.CompilerParams(dimension_semantics=("arbitrary",), vmem_limit_bytes=VMEM_LIMIT),
        name="dsa_sample_select",
    )(page_table, qi64, wcol, ki_new, cache_k_idx, uo)


def _dsa_sample_attn_kernel(pt_ref, qa_ref, sel_ref, bias_ref, cb_ref, kn_ref, vn_ref, kcache, vcache, o_ref,
                            kbuf, vbuf, sem, lg_ref, new_ref, *, n_pages, page):
    n = pl.program_id(0)
    nseq = pl.num_programs(0)
    slot = n % 2
    past = n_pages * page
    n_chunks = past // KEY_CHUNK
    rows = N_HEADS_A * 4

    def start(seq, sl):
        _start_pages(kcache, kbuf.at[sl], sem.at[0, sl], pt_ref, seq, 0, n_pages, page)
        _start_pages(vcache, vbuf.at[sl], sem.at[1, sl], pt_ref, seq, 0, n_pages, page)

    @pl.when(n == 0)
    def _():
        start(n, slot)

    @pl.when(n + 1 < nseq)
    def _():
        start(n + 1, 1 - slot)

    lo_half = lax.broadcasted_iota(I32, (rows // 2, LANES), 1) < HEAD_DIM
    q = qa_ref[...].astype(F32)
    q32 = jnp.concatenate([jnp.where(lo_half, q, 0.0), jnp.where(lo_half, 0.0, q)], axis=0).astype(BF16)

    r = lax.broadcasted_iota(I32, (rows, SEQ_GROUP * 4), 0)
    c = lax.broadcasted_iota(I32, (rows, SEQ_GROUP * 4), 1)
    pick = jnp.where(c == (n % SEQ_GROUP) * 4 + (r % 16) // 4, 1.0, 0.0).astype(BF16)
    sel = _dot(pick, sel_ref[...].astype(BF16))

    cb = cb_ref[...]
    bias = bias_ref[...]
    _wait_pages(kcache, kbuf.at[slot], sem.at[0, slot], n_pages, page)
    mx = jnp.full((rows, LANES), NEG, F32)
    for ch in range(n_chunks):
        kc = kbuf[slot, :, ch * KEY_CHUNK:(ch + 1) * KEY_CHUNK].astype(BF16)
        lg = _dot(q32, kc) * ATTN_SCALE + cb + sel[:, ch * KEY_CHUNK:(ch + 1) * KEY_CHUNK]
        if ch == n_chunks - 1:
            near = jnp.concatenate([jnp.zeros((rows, KEY_CHUNK - LANES), F32), bias[:, :LANES] - cb], axis=1)
            lg = lg + near
        lg_ref[ch] = lg
        for t in range(KEY_CHUNK // LANES):
            mx = jnp.maximum(mx, lg[:, t * LANES:(t + 1) * LANES])
    new_ref[...] = jnp.zeros(new_ref.shape, F32)
    new_ref[0:4, :] = kn_ref[0]
    lg_new = _dot_nt(q32, new_ref[...].astype(BF16)) * ATTN_SCALE + bias[:, LANES:] + sel[:, past:]
    m = jnp.max(jnp.maximum(mx, lg_new), axis=1, keepdims=True)

    _wait_pages(vcache, vbuf.at[slot], sem.at[1, slot], n_pages, page)
    new_ref[0:4, :] = vn_ref[0]
    p_new = jnp.exp(lg_new - m)
    out = _dot(p_new.astype(BF16), new_ref[...].astype(BF16))
    lsum = p_new
    for ch in range(n_chunks):
        pr = jnp.exp(lg_ref[ch] - m)
        out = out + _dot_nt(pr.astype(BF16), vbuf[slot, :, ch * KEY_CHUNK:(ch + 1) * KEY_CHUNK].astype(BF16))
        for t in range(KEY_CHUNK // LANES):
            lsum = lsum + pr[:, t * LANES:(t + 1) * LANES]
    out = out / jnp.sum(lsum, axis=1, keepdims=True)
    o_ref[...] = jnp.where(lo_half, out[:rows // 2], out[rows // 2:]).astype(BF16)


def _dsa_sample_attn_call(page_table, qa16, sel, bias32, cb32, ka_new, va_new, cache_k_a, cache_v_a):
    n, n_pages = page_table.shape
    page = cache_k_a.shape[-1]
    past = n_pages * page
    width = past + LANES
    rows = N_HEADS_A * 4
    kv_w = N_KV_A * HEAD_DIM
    return pl.pallas_call(
        functools.partial(_dsa_sample_attn_kernel, n_pages=n_pages, page=page),
        grid_spec=pltpu.PrefetchScalarGridSpec(
            num_scalar_prefetch=1,
            grid=(n,),
            in_specs=[pl.BlockSpec((rows // 2, LANES), lambda s, pt: (s, 0)),
                      pl.BlockSpec((SEQ_GROUP * 4, width), lambda s, pt: (s // SEQ_GROUP, 0)),
                      pl.BlockSpec(bias32.shape, lambda s, pt: (0, 0)),
                      pl.BlockSpec(cb32.shape, lambda s, pt: (0, 0)),
                      pl.BlockSpec((1, 4, kv_w), lambda s, pt: (s, 0, 0)),
                      pl.BlockSpec((1, 4, kv_w), lambda s, pt: (s, 0, 0)),
                      pl.BlockSpec(memory_space=pl.ANY),
                      pl.BlockSpec(memory_space=pl.ANY)],
            out_specs=pl.BlockSpec((rows // 2, LANES), lambda s, pt: (s, 0)),
            scratch_shapes=[pltpu.VMEM((2, kv_w, past), F32),
                            pltpu.VMEM((2, kv_w, past), F32),
                            pltpu.SemaphoreType.DMA((2, 2)),
                            pltpu.VMEM((past // KEY_CHUNK, rows, KEY_CHUNK), F32),
                            pltpu.VMEM((LANES, kv_w), F32)]),
        out_shape=jax.ShapeDtypeStruct((n * rows // 2, LANES), BF16),
        compiler_params=pltpu.CompilerParams(dimension_semantics=("arbitrary",), vmem_limit_bytes=VMEM_LIMIT),
        name="dsa_sample_attn",
    )(page_table, qa16, sel, bias32, cb32, ka_new, va_new, cache_k_a, cache_v_a)


def _sb_sample_kernel(pt_ref, qb_ref, kn_ref, vn_ref, kcache, vcache, to_ref, o_ref,
                      kbuf, vbuf, sem, qbd_ref, acc_ref, oacc_ref, new_ref, *, n_pages, page, chunk):
    n = pl.program_id(0)
    c = pl.program_id(1)
    n_chunks = n_pages * page // chunk
    step = n * n_chunks + c
    slot = step % 2
    pages_per_chunk = chunk // page
    rows = N_HEADS_B * 4
    width = N_HEADS_B * HEAD_DIM
    tiles = chunk // LANES

    def start(st, sl):
        seq = st // n_chunks
        first = n_pages - (st % n_chunks + 1) * pages_per_chunk
        _start_pages(kcache, kbuf.at[sl], sem.at[0, sl], pt_ref, seq, first, pages_per_chunk, page)
        _start_pages(vcache, vbuf.at[sl], sem.at[1, sl], pt_ref, seq, first, pages_per_chunk, page)

    @pl.when(step == 0)
    def _():
        start(step, slot)

    @pl.when(step + 1 < pl.num_programs(0) * n_chunks)
    def _():
        start(step + 1, 1 - slot)

    @pl.when(c == 0)
    def _():
        q = qb_ref[0].astype(F32)
        q32 = jnp.concatenate([jnp.broadcast_to(q[t:t + 1, :], (N_HEADS_B, width)) for t in range(4)], axis=0)
        r = lax.broadcasted_iota(I32, (rows, width), 0)
        l = lax.broadcasted_iota(I32, (rows, width), 1)
        qbd_ref[...] = jnp.where(l // HEAD_DIM == r % N_HEADS_B, q32, 0.0).astype(BF16)
        row = lax.broadcasted_iota(I32, (rows, LANES), 0)
        lane = lax.broadcasted_iota(I32, (rows, LANES), 1)
        mask = jnp.logical_and(lane < row // N_HEADS_B, lane < 4)
        new_ref[...] = jnp.zeros(new_ref.shape, F32)
        new_ref[0:4, :] = kn_ref[0]
        z = _dot_nt(qbd_ref[...], new_ref[...].astype(BF16)) * ATTN_SCALE
        sp = _softplus(z)
        hi, lo = _split_bf16(jnp.where(mask, -sp, 0.0))
        cs = _dot(hi, to_ref[...]) + _dot(lo, to_ref[...])
        a = jnp.where(mask, jnp.exp(z - sp + cs[:, :LANES]), 0.0)
        new_ref[0:4, :] = vn_ref[0]
        oacc_ref[...] = _dot(a.astype(BF16), new_ref[...].astype(BF16))
        acc_ref[...] = cs[:, LANES:]

    _wait_pages(kcache, kbuf.at[slot], sem.at[0, slot], pages_per_chunk, page)
    z = _dot(qbd_ref[...], kbuf[slot].astype(BF16)) * ATTN_SCALE
    sp = _softplus(z)
    stack = jnp.concatenate([-sp[:, t * LANES:(t + 1) * LANES] for t in range(tiles)], axis=0)
    hi, lo = _split_bf16(stack)
    cs = _dot(hi, to_ref[...]) + _dot(lo, to_ref[...])
    run = acc_ref[...]
    parts = [None] * tiles
    for t in reversed(range(tiles)):
        after = cs[t * rows:(t + 1) * rows, :LANES] + run
        run = run + cs[t * rows:(t + 1) * rows, LANES:]
        sl = slice(t * LANES, (t + 1) * LANES)
        parts[t] = jnp.exp(z[:, sl] - sp[:, sl] + after)
    acc_ref[...] = run
    a = jnp.concatenate(parts, axis=1).astype(BF16)
    _wait_pages(vcache, vbuf.at[slot], sem.at[1, slot], pages_per_chunk, page)
    oacc_ref[...] += _dot_nt(a, vbuf[slot].astype(BF16))

    @pl.when(c == n_chunks - 1)
    def _():
        r = lax.broadcasted_iota(I32, (rows, width), 0)
        l = lax.broadcasted_iota(I32, (rows, width), 1)
        diag = jnp.where(l // HEAD_DIM == r % N_HEADS_B, oacc_ref[...], 0.0)
        o_ref[0] = jnp.sum(diag.reshape(4, N_HEADS_B, width), axis=1)


def _sb_sample_call(page_table, qb4, kb_new, vb_new, cache_k_b, cache_v_b, to):
    n, n_pages = page_table.shape
    page = cache_k_b.shape[-1]
    width = N_HEADS_B * HEAD_DIM
    rows = N_HEADS_B * 4
    tok = pl.BlockSpec((1, 4, width), lambda s, c, pt: (s, 0, 0))
    past = n_pages * page
    chunk = SB_CHUNK if past % SB_CHUNK == 0 and past > SB_CHUNK else KEY_CHUNK
    return pl.pallas_call(
        functools.partial(_sb_sample_kernel, n_pages=n_pages, page=page, chunk=chunk),
        grid_spec=pltpu.PrefetchScalarGridSpec(
            num_scalar_prefetch=1,
            grid=(n, past // chunk),
            in_specs=[tok, tok, tok,
                      pl.BlockSpec(memory_space=pl.ANY), pl.BlockSpec(memory_space=pl.ANY),
                      pl.BlockSpec(to.shape, lambda s, c, pt: (0, 0))],
            out_specs=tok,
            scratch_shapes=[pltpu.VMEM((2, width, chunk), F32),
                            pltpu.VMEM((2, width, chunk), F32),
                            pltpu.SemaphoreType.DMA((2, 2)),
                            pltpu.VMEM((rows, width), BF16),
                            pltpu.VMEM((rows, LANES), F32),
                            pltpu.VMEM((rows, width), F32),
                            pltpu.VMEM((LANES, width), F32)]),
        out_shape=jax.ShapeDtypeStruct((n, 4, width), F32),
        compiler_params=pltpu.CompilerParams(dimension_semantics=("arbitrary", "arbitrary"),
                                             vmem_limit_bytes=VMEM_LIMIT),
        name="sb_sample",
    )(page_table, qb4, kb_new, vb_new, cache_k_b, cache_v_b, to)


def _prep_weights(w_in, w_o_a, w_o_b, w_out, w_up1, w_down1, w_up2, w_down2):
    w = w_in[0]
    d_model = w.shape[0]
    width_a = N_HEADS_A * HEAD_DIM
    kv_a = N_KV_A * HEAD_DIM
    width_i = N_IDX_HEADS * IDX_DIM
    width_b = N_HEADS_B * HEAD_DIM
    sizes = (width_a, kv_a, kv_a, width_i, IDX_DIM, N_IDX_HEADS, width_b, width_b, width_b, d_model, d_model)
    offs = np.cumsum((0,) + sizes)
    q_a, k_a, v_a, q_i, k_i, w_i, q_b, k_b, v_b, g_a, g_b = [w[:, int(offs[n]):int(offs[n + 1])] for n in range(len(sizes))]
    perm = np.concatenate([np.arange(HEAD_DIM) + (g * GROUP_A + j) * HEAD_DIM
                           for j in range(GROUP_A) for g in range(N_KV_A)])
    pad = jnp.zeros((d_model, LANES - IDX_DIM - N_IDX_HEADS), w.dtype)
    proj = [q_a[:, perm], k_a, v_a, q_i, jnp.concatenate([k_i, w_i, pad], axis=1),
            jnp.concatenate([k_i, k_i], axis=1), q_b, k_b, v_b, g_a, g_b]
    bf = lambda a: a.astype(BF16)
    kw = proj[4]
    return dict(
        proj=[bf(p) for p in proj],
        proj_rows=[bf(p) for p in (proj[0], k_a, q_i, proj[5], q_b, k_b, g_a, g_b)],
        proj_cols=bf(jnp.concatenate([k_a, v_a, kw, k_b, v_b], axis=1).T),
        woa=bf(w_o_a[0][perm, :]), wob=bf(w_o_b[0]), wout=bf(w_out[0]),
        wup1=bf(w_up1[0]), wdn1=bf(w_down1[0]), wup2=bf(w_up2[0]), wdn2=bf(w_down2[0]),
    )


def _prompt_tables(rel_bias):
    key = np.arange(LANES)[:, None]
    query = np.arange(Q_TILE)[None, :]
    idx = np.concatenate([_t5_bucket_np(rel * LANES + query - key) for rel in range(3)], axis=0)
    return _bias_call(rel_bias, idx).reshape(rel_bias.shape[1], 3, LANES, Q_TILE)


def _prompt_path(xp, wts, rel_bias, g_ffn1, g_mix, g_ffn2, g_final, batch, seq):
    assert seq % SUPER == 0
    x1 = _ffn_call(xp, g_ffn1, wts["wup1"], wts["wdn1"])
    pr = _proj_prompt_call(x1, g_mix, wts["proj_rows"], wts["proj_cols"], batch, seq)
    oa = _dsa_prompt_call(pr, pr["kwt"], pr["vatb"], _prompt_tables(rel_bias), batch, seq)
    ob = _sb_prompt_call(pr, pr["vbtb"], _tri_lower_ones(), batch, seq)
    y = _post_call(x1, oa, ob, pr["sa"], pr["sb"], wts["woa"], wts["wob"], wts["wout"],
                   g_ffn2, wts["wup2"], wts["wdn2"], g_final)
    return y, pr


def _sample_tables(rel_bias):
    t = np.arange(4)[:, None]
    c = np.arange(LANES)[None, :]
    idx = np.concatenate([_t5_bucket_np(LANES + t - c), _t5_bucket_np(t - c)], axis=1)
    bias = _bias_call(rel_bias, idx)
    bias32 = bias.reshape(N_KV_A, GROUP_A, 4, 2 * LANES).transpose(0, 2, 1, 3).reshape(N_HEADS_A * 4, 2 * LANES)
    far = rel_bias[N_BUCKETS - 1].reshape(N_KV_A, 1, GROUP_A, 1)
    cb32 = jnp.broadcast_to(far, (N_KV_A, 4, GROUP_A, 1)).reshape(N_HEADS_A * 4, 1)
    return bias32, cb32


def _sample_path(xs, wts, rel_bias, caches, page_table, g_ffn1, g_mix, g_ffn2, g_final, n_seq, n_tok):
    assert n_tok == 4 and n_seq % SEQ_GROUP == 0
    n_pages = page_table.shape[1]
    page = caches[2].shape[2]

    def key_minor(c):
        nd = c.ndim
        c = jnp.transpose(c, (0, 1) + tuple(range(3, nd)) + (2,))
        return c.reshape(c.shape[:2] + (-1, page))
    cache_k_a, cache_v_a, cache_k_idx, cache_k_b, cache_v_b = [key_minor(c) for c in caches]
    assert (n_pages * page) % KEY_CHUNK == 0 and KEY_CHUNK % page == 0
    x1 = _ffn_call(xs, g_ffn1, wts["wup1"], wts["wdn1"])
    pr = _proj_call(x1, g_mix, wts["proj"])
    topk = min(TOPK_MAX, (n_pages * page + n_tok) // 4)
    half = n_seq // SEQ_GROUP
    qi64 = pr["qi"].reshape(half, SEQ_GROUP, 4, N_IDX_HEADS, IDX_DIM).transpose(0, 3, 1, 2, 4)
    qi64 = qi64.reshape(half * N_IDX_HEADS * SEQ_GROUP * 4, IDX_DIM)
    wcol = pr["kw"][:, IDX_DIM:IDX_DIM + N_IDX_HEADS].reshape(half, SEQ_GROUP, 4, N_IDX_HEADS).transpose(0, 3, 1, 2)
    wcol = wcol.reshape(half * N_IDX_HEADS * SEQ_GROUP * 4, 1)
    ki_new = pr["kw"][:, :IDX_DIM].reshape(n_seq, 4, IDX_DIM)
    sel = _dsa_sample_select_call(page_table, qi64, wcol, ki_new, cache_k_idx, _tri_ones(True), topk)
    bias32, cb32 = _sample_tables(rel_bias)
    oa = _dsa_sample_attn_call(page_table, pr["qa"].reshape(n_seq * 16, LANES), sel, bias32, cb32,
                               pr["ka"].reshape(n_seq, 4, -1), pr["va"].reshape(n_seq, 4, -1), cache_k_a, cache_v_a)
    ob = _sb_sample_call(page_table, pr["qb"].reshape(n_seq, 4, -1), pr["kb"].reshape(n_seq, 4, -1),
                         pr["vb"].reshape(n_seq, 4, -1), cache_k_b, cache_v_b, _tri_ones(False))
    oa = oa.reshape(n_seq * 4, -1)
    ob = ob.reshape(n_seq * 4, -1).astype(BF16)
    y = _post_call(x1, oa, ob, pr["sa"], pr["sb"], wts["woa"], wts["wob"], wts["wout"],
                   g_ffn2, wts["wup2"], wts["wdn2"], g_final)
    return y, pr


def _prompt_rows(pr, batch, seq):
    def heads(a, n):
        return a.reshape(batch, n, HEAD_DIM, seq).transpose(0, 3, 1, 2)[None]
    return (heads(pr["kat"], N_KV_A), heads(pr["vat"], N_KV_A),
            pr["kwt"][:, :IDX_DIM, :].transpose(0, 2, 1)[None],
            heads(pr["kbt"], N_HEADS_B), heads(pr["vbt"], N_HEADS_B))


def _rows(pr, lead):
    depth = (1,)
    return (pr["ka"].reshape(depth + lead + (N_KV_A, HEAD_DIM)),
            pr["va"].reshape(depth + lead + (N_KV_A, HEAD_DIM)),
            pr["kw"][:, :IDX_DIM].reshape(depth + lead + (IDX_DIM,)),
            pr["kb"].reshape(depth + lead + (N_HEADS_B, HEAD_DIM)),
            pr["vb"].reshape(depth + lead + (N_HEADS_B, HEAD_DIM)))


def kernel(x_prompt, x_sample, cache_k_a, cache_v_a, cache_k_idx, cache_k_b, cache_v_b, page_table,
           w_in, w_o_a, w_o_b, w_out, rel_bias, g_ffn1, w_up1, w_down1, g_mix, g_ffn2, w_up2, w_down2, g_final):
    assert w_in.shape[0] == 1, "single-layer step"
    batch, seq, d_model = x_prompt.shape
    n_seq, n_tok, _ = x_sample.shape
    assert seq % Q_TILE == 0
    wts = _prep_weights(w_in, w_o_a, w_o_b, w_out, w_up1, w_down1, w_up2, w_down2)
    gf = g_final.reshape(1, d_model)
    yp, prp = _prompt_path(x_prompt.reshape(batch * seq, d_model), wts, rel_bias,
                           g_ffn1, g_mix, g_ffn2, gf, batch, seq)
    ys, prs = _sample_path(x_sample.reshape(n_seq * n_tok, d_model), wts, rel_bias,
                           (cache_k_a, cache_v_a, cache_k_idx, cache_k_b, cache_v_b), page_table,
                           g_ffn1, g_mix, g_ffn2, gf, n_seq, n_tok)
    return ((yp.reshape(batch, seq, d_model), ys.reshape(n_seq, n_tok, d_model))
            + _prompt_rows(prp, batch, seq) + _rows(prs, (n_seq, n_tok)))
```

```python
import functools
import math

import jax
import jax.numpy as jnp
import numpy as np
from jax import lax
from jax.experimental import pallas as pl
from jax.experimental.pallas import tpu as pltpu

F32 = jnp.float32
BF16 = jnp.bfloat16
I32 = jnp.int32

HEAD_DIM = 64
IDX_DIM = 64
N_HEADS_A = 8
N_KV_A = 2
GROUP_A = N_HEADS_A // N_KV_A
N_IDX_HEADS = 8
N_HEADS_B = 8
TOPK_MAX = 256
N_BUCKETS = 32
MAX_DISTANCE = 128
EPS = 1e-6
ATTN_SCALE = HEAD_DIM ** -0.5
IDX_SCALE = IDX_DIM ** -0.5
IDX_HEAD_SCALE = N_IDX_HEADS ** -0.5

LANES = 128
Q_TILE = 128
NEG = -1e30
INT_MIN = -2 ** 31
VMEM_LIMIT = 56 * 1024 * 1024
FFN_CHUNK = 256
TOKEN_TILE = 512
SUPER = 512
SUB = SUPER // LANES
TL_ROWS = LANES + 16


def _pick_tile(n, pref):
    t = min(n, pref)
    while n % t or t % 8:
        t -= 1
    return t


def _const_spec(shape):
    nd = len(shape)
    return pl.BlockSpec(shape, lambda *_: (0,) * nd, pipeline_mode=pl.Buffered(1))


def _dot(a, b):
    return jnp.dot(a, b, preferred_element_type=F32)


def _dot_nt(a, b):
    return lax.dot_general(a, b, (((1,), (1,)), ((), ())), preferred_element_type=F32)


def _rms(x, g):
    r = lax.rsqrt(jnp.mean(x * x, axis=-1, keepdims=True) + EPS)
    return (x * r) * g


def _softplus(z):
    return jnp.maximum(z, 0.0) + jnp.log(1.0 + jnp.exp(-jnp.abs(z)))


def _split_bf16(x):
    hi = x.astype(BF16)
    lo = (x - hi.astype(F32)).astype(BF16)
    return hi, lo


KEY_NEG_INF = INT_MIN + 0x7FFFFF


def _key_to_float(key):
    key = jnp.maximum(key, KEY_NEG_INF)
    return lax.bitcast_convert_type(key ^ ((key >> 31) & jnp.int32(0x7FFFFFFF)), F32)


def _search_threshold(count_ge, topk, shape):
    def bit_step(it, u):
        bit = lax.shift_left(jnp.int32(1), 31 - it)
        cand = _key_to_float((u | bit) ^ INT_MIN)
        return jnp.where(count_ge(cand) >= topk, u | bit, u)
    u = lax.fori_loop(0, 32, bit_step, jnp.zeros(shape, I32))
    return _key_to_float(u ^ INT_MIN)


def _swiglu(x, g_ref, wup_ref, wdn_ref, act_ref):
    d_ff = wdn_ref.shape[0]
    h = _rms(x, g_ref[...]).astype(BF16)
    for c in range(d_ff // FFN_CHUNK):
        lo = c * FFN_CHUNK
        gate = _dot(h, wup_ref[:, lo:lo + FFN_CHUNK])
        up = _dot(h, wup_ref[:, d_ff + lo:d_ff + lo + FFN_CHUNK])
        act_ref[:, lo:lo + FFN_CHUNK] = (gate * jax.nn.sigmoid(gate) * up).astype(BF16)
    return _dot(act_ref[...], wdn_ref[...])


def _ffn_kernel(x_ref, g_ref, wup_ref, wdn_ref, o_ref, act_ref):
    x = x_ref[...]
    o_ref[...] = x + 0.5 * _swiglu(x, g_ref, wup_ref, wdn_ref, act_ref)


def _ffn_call(x, g, wup, wdn):
    n, d = x.shape
    d_ff = wdn.shape[0]
    tm = _pick_tile(n, TOKEN_TILE)
    row = lambda w: pl.BlockSpec((tm, w), lambda i: (i, 0))
    return pl.pallas_call(
        _ffn_kernel,
        grid=(n // tm,),
        in_specs=[row(d), _const_spec(g.shape), _const_spec(wup.shape), _const_spec(wdn.shape)],
        out_specs=row(d),
        out_shape=jax.ShapeDtypeStruct((n, d), F32),
        scratch_shapes=[pltpu.VMEM((tm, d_ff), BF16)],
        compiler_params=pltpu.CompilerParams(dimension_semantics=("arbitrary",), vmem_limit_bytes=VMEM_LIMIT),
        name="ffn_pre",
    )(x, g, wup, wdn)


def _proj_kernel(x_ref, g_ref, wqa, wka, wva, wqi, wkw, wkk, wqb, wkb, wvb, wga, wgb,
                 qa, ka, va, kab, vab, qi, kw, kk, qb, kb, vb, kbb, vbb, sa, sb):
    h = _rms(x_ref[...], g_ref[...]).astype(BF16)
    qa[...] = _dot(h, wqa[...]).astype(BF16)
    k = _dot(h, wka[...])
    ka[...] = k
    kab[...] = k.astype(BF16)
    v = _dot(h, wva[...])
    va[...] = v
    vab[...] = v.astype(BF16)
    qi[...] = _dot(h, wqi[...]).astype(BF16)
    kw[...] = _dot(h, wkw[...])
    kk[...] = _dot(h, wkk[...]).astype(BF16)
    qb[...] = _dot(h, wqb[...]).astype(BF16)
    k = _dot(h, wkb[...])
    kb[...] = k
    kbb[...] = k.astype(BF16)
    v = _dot(h, wvb[...])
    vb[...] = v
    vbb[...] = v.astype(BF16)
    sa[...] = jax.nn.sigmoid(_dot(h, wga[...]))
    sb[...] = jax.nn.sigmoid(_dot(h, wgb[...]))


_PROJ_OUT = (("qa", 512, BF16), ("ka", 128, F32), ("va", 128, F32), ("kab", 128, BF16), ("vab", 128, BF16),
             ("qi", 512, BF16), ("kw", 128, F32), ("kk", 128, BF16), ("qb", 512, BF16),
             ("kb", 512, F32), ("vb", 512, F32), ("kbb", 512, BF16), ("vbb", 512, BF16),
             ("sa", 1024, F32), ("sb", 1024, F32))


def _proj_call(x, g, weights):
    n, d = x.shape
    tm = _pick_tile(n, TOKEN_TILE)
    row = lambda w: pl.BlockSpec((tm, w), lambda i: (i, 0))
    outs = pl.pallas_call(
        _proj_kernel,
        grid=(n // tm,),
        in_specs=[row(d), _const_spec(g.shape)] + [_const_spec(w.shape) for w in weights],
        out_specs=[row(w) for _, w, _ in _PROJ_OUT],
        out_shape=[jax.ShapeDtypeStruct((n, w), dt) for _, w, dt in _PROJ_OUT],
        compiler_params=pltpu.CompilerParams(dimension_semantics=("arbitrary",), vmem_limit_bytes=VMEM_LIMIT),
        name="proj",
    )(x, g, *weights)
    return dict(zip([nm for nm, _, _ in _PROJ_OUT], outs))


_PROMPT_ROW_OUT = (("qa", 512, BF16), ("kab", 128, BF16), ("qi", 512, BF16), ("kk", 128, BF16),
                   ("qb", 512, BF16), ("kbb", 512, BF16), ("sa", 1024, F32), ("sb", 1024, F32))
_PROMPT_COL_OUT = (("kat", 0, 128), ("vat", 128, 128), ("kwt", 256, 128), ("kbt", 384, 512), ("vbt", 896, 512))


def _proj_prompt_kernel(x_ref, g_ref, wqa, wka, wqi, wkk, wqb, wkb, wga, wgb, wt_ref,
                        qa, kab, qi, kk, qb, kbb, sa, sb, kat, vat, kwt, kbt, vbt, vatb, vbtb):
    h = _rms(x_ref[...], g_ref[...]).astype(BF16)
    qa[...] = _dot(h, wqa[...]).astype(BF16)
    kab[...] = _dot(h, wka[...]).astype(BF16)
    qi[...] = _dot(h, wqi[...]).astype(BF16)
    kk[...] = _dot(h, wkk[...]).astype(BF16)
    qb[...] = _dot(h, wqb[...]).astype(BF16)
    kbb[...] = _dot(h, wkb[...]).astype(BF16)
    sa[...] = jax.nn.sigmoid(_dot(h, wga[...]))
    sb[...] = jax.nn.sigmoid(_dot(h, wgb[...]))
    t = _dot_nt(wt_ref[...], h)
    for ref, (_, r0, nr) in zip((kat, vat, kwt, kbt, vbt), _PROMPT_COL_OUT):
        ref[0] = t[r0:r0 + nr]
    vatb[0, 0] = t[128:256].astype(BF16)
    vbtb[0, 0] = t[896:1408].astype(BF16)


def _proj_prompt_call(x, g, weights, wt_all, batch, seq):
    n, d = x.shape
    tm = SUPER
    per_b = seq // tm
    row = lambda w: pl.BlockSpec((tm, w), lambda i: (i, 0))
    col = lambda r: pl.BlockSpec((1, r, tm), lambda i: (i // per_b, 0, i % per_b))
    blk = lambda r: pl.BlockSpec((1, 1, r, tm), lambda i: (i // per_b, i % per_b, 0, 0))
    out_specs = ([row(w) for _, w, _ in _PROMPT_ROW_OUT] + [col(nr) for _, _, nr in _PROMPT_COL_OUT]
                 + [blk(128), blk(512)])
    out_shape = ([jax.ShapeDtypeStruct((n, w), dt) for _, w, dt in _PROMPT_ROW_OUT]
                 + [jax.ShapeDtypeStruct((batch, nr, seq), F32) for _, _, nr in _PROMPT_COL_OUT]
                 + [jax.ShapeDtypeStruct((batch, per_b, 128, tm), BF16),
                    jax.ShapeDtypeStruct((batch, per_b, 512, tm), BF16)])
    outs = pl.pallas_call(
        _proj_prompt_kernel,
        grid=(n // tm,),
        in_specs=[row(d), _const_spec(g.shape)] + [_const_spec(w.shape) for w in weights] + [_const_spec(wt_all.shape)],
        out_specs=out_specs,
        out_shape=out_shape,
        compiler_params=pltpu.CompilerParams(dimension_semantics=("arbitrary",), vmem_limit_bytes=VMEM_LIMIT),
        name="proj_prompt",
    )(x, g, *weights, wt_all)
    names = [nm for nm, _, _ in _PROMPT_ROW_OUT] + [nm for nm, _, _ in _PROMPT_COL_OUT] + ["vatb", "vbtb"]
    return dict(zip(names, outs))


def _post_kernel(x_ref, oa_ref, ob_ref, sa_ref, sb_ref, woa, wob, wout, g2_ref, wup_ref, wdn_ref, gf_ref,
                 y_ref, act_ref):
    mix = sa_ref[...] * _dot(oa_ref[...], woa[...]) + sb_ref[...] * _dot(ob_ref[...], wob[...])
    x2 = x_ref[...] + _dot(mix.astype(BF16), wout[...])
    x3 = x2 + 0.5 * _swiglu(x2, g2_ref, wup_ref, wdn_ref, act_ref)
    y_ref[...] = _rms(x3, gf_ref[...])


def _post_call(x, oa, ob, sa, sb, woa, wob, wout, g2, wup, wdn, gf):
    n, d = x.shape
    d_ff = wdn.shape[0]
    tm = _pick_tile(n, TOKEN_TILE)
    row = lambda w: pl.BlockSpec((tm, w), lambda i: (i, 0))
    consts = (woa, wob, wout, g2, wup, wdn, gf)
    return pl.pallas_call(
        _post_kernel,
        grid=(n // tm,),
        in_specs=[row(d), row(oa.shape[1]), row(ob.shape[1]), row(d), row(d)] + [_const_spec(c.shape) for c in consts],
        out_specs=row(d),
        out_shape=jax.ShapeDtypeStruct((n, d), F32),
        scratch_shapes=[pltpu.VMEM((tm, d_ff), BF16)],
        compiler_params=pltpu.CompilerParams(dimension_semantics=("arbitrary",), vmem_limit_bytes=VMEM_LIMIT),
        name="post",
    )(x, oa, ob, sa, sb, *consts)


def _t5_bucket_np(dist):
    max_exact = N_BUCKETS // 2
    d = np.maximum(dist, 0)
    ratio = np.maximum(d, 1).astype(np.float32) / np.float32(max_exact)
    large = max_exact + (np.log(ratio) / np.float32(math.log(MAX_DISTANCE / max_exact))
                         * np.float32(N_BUCKETS - max_exact)).astype(np.int32)
    large = np.minimum(large, N_BUCKETS - 1)
    return np.where(d < max_exact, d, large).astype(np.int32)


def _bias_kernel(rel_ref, idx_ref, out_ref):
    idx = idx_ref[...]
    for h in range(out_ref.shape[0]):
        acc = jnp.zeros(idx.shape, F32)
        for b in range(N_BUCKETS):
            acc = jnp.where(idx == b, rel_ref[b, h], acc)
        out_ref[h] = acc


def _bias_call(rel_bias, idx):
    return pl.pallas_call(
        _bias_kernel,
        in_specs=[pl.BlockSpec(memory_space=pltpu.SMEM), pl.BlockSpec(memory_space=pltpu.VMEM)],
        out_specs=pl.BlockSpec(memory_space=pltpu.VMEM),
        out_shape=jax.ShapeDtypeStruct((rel_bias.shape[1],) + idx.shape, F32),
        name="rel_bias_table",
    )(rel_bias, jnp.asarray(idx))


def _half_masks():
    lane = lax.broadcasted_iota(I32, (Q_TILE, LANES), 1)
    return lane < HEAD_DIM


def _store_masked_pairs(src_ref, dst_ref, n_pairs, scale=1.0):
    lo_half = _half_masks()
    for p in range(n_pairs):
        pair = src_ref[:, p * LANES:(p + 1) * LANES].astype(F32) * scale
        dst_ref[p, 0:Q_TILE, :] = jnp.where(lo_half, pair, 0.0).astype(BF16)
        dst_ref[p, Q_TILE:2 * Q_TILE, :] = jnp.where(lo_half, 0.0, pair).astype(BF16)


def _tri_ones(strict_upper):
    r = np.arange(LANES)
    if strict_upper:
        tri = (r[:, None] < r[None, :])
    else:
        tri = (r[:, None] > r[None, :])
    return jnp.asarray(np.concatenate([tri, np.ones((LANES, LANES), bool)], axis=1), dtype=BF16)


def _dsa_prompt_kernel(qi_ref, kwt_ref, kk_ref, qa_ref, ka_ref, vat_ref, bias_ref, lt_ref, o_ref,
                       qim_ref, qam_ref, key_ref, sel_ref, lg_ref, mx_ref, ls_ref, out_ref, *, topk):
    i = pl.program_id(1)
    ntile = i + 1
    n_super = i // SUB + 1
    n_pairs = N_HEADS_A // 2
    row = lax.broadcasted_iota(I32, (LANES, LANES), 0)
    lane = lax.broadcasted_iota(I32, (LANES, LANES), 1)
    key_minus_query = row - lane

    _store_masked_pairs(qi_ref, qim_ref, N_IDX_HEADS // 2, IDX_SCALE)
    _store_masked_pairs(qa_ref, qam_ref, n_pairs, ATTN_SCALE)
    wt = kwt_ref[0, IDX_DIM:IDX_DIM + N_IDX_HEADS, :] * IDX_HEAD_SCALE
    mx_ref[...] = jnp.full(mx_ref.shape, NEG, F32)
    ls_ref[...] = jnp.zeros(ls_ref.shape, F32)
    out_ref[...] = jnp.zeros(out_ref.shape, F32)

    def super_rows(jt):
        return pl.ds(pl.multiple_of(jt * SUPER, SUPER), SUPER)

    def valid_tile(j):
        return key_minus_query <= (i - j) * LANES

    def score_super(jt, c):
        kt = kk_ref[super_rows(jt), :]
        acc = jnp.zeros((SUPER, LANES), F32)
        for p in range(N_IDX_HEADS // 2):
            s = _dot_nt(kt, qim_ref[p])
            acc = acc + wt[2 * p:2 * p + 1] * jnp.maximum(s[:, :LANES], 0.0)
            acc = acc + wt[2 * p + 1:2 * p + 2] * jnp.maximum(s[:, LANES:], 0.0)
        for k in range(SUB):
            j = jt * SUB + k
            key_ref[j] = jnp.where(valid_tile(j), acc[k * LANES:(k + 1) * LANES], -jnp.inf)
        return c
    lax.fori_loop(0, n_super, score_super, 0)

    def count_tiles(pred):
        def body(jt, cnt):
            for k in range(SUB):
                cnt = cnt + jnp.where(pred(key_ref[jt * SUB + k]), 1, 0)
            return cnt
        cnt = lax.fori_loop(0, n_super, body, jnp.zeros((LANES, LANES), I32))
        return jnp.sum(cnt, axis=0, keepdims=True)

    thr = _search_threshold(lambda cand: count_tiles(lambda k: k >= cand), topk, (1, LANES))
    need = (topk - count_tiles(lambda k: k > thr)).astype(F32)

    over = jnp.logical_and(count_tiles(lambda k: k >= thr) > topk, thr > -jnp.inf)
    ties = jnp.sum(jnp.where(over, 1, 0)) > 0

    @pl.when(ties)
    def _():
        def select_tile(j, carry):
            key = key_ref[j]
            eq = key == thr
            cs = _dot(lt_ref[...], jnp.where(eq, 1.0, 0.0).astype(BF16))
            take = jnp.logical_or(key > thr, jnp.logical_and(eq, cs[:LANES] + carry < need))
            sel_ref[j] = jnp.where(jnp.logical_and(take, valid_tile(j)), 0.0, NEG)
            return carry + cs[LANES:LANES + 1]
        lax.fori_loop(0, n_super * SUB, select_tile, jnp.zeros((1, LANES), F32))

    @pl.when(jnp.logical_not(ties))
    def _():
        def select_tile(j, c):
            take = jnp.logical_and(key_ref[j] >= thr, valid_tile(j))
            sel_ref[j] = jnp.where(take, 0.0, NEG)
            return c
        lax.fori_loop(0, n_super * SUB, select_tile, 0)

    def logit_super(jt, c):
        kt = ka_ref[super_rows(jt), :]
        for p in range(n_pairs):
            s = _dot_nt(kt, qam_ref[p])
            mx = mx_ref[p]
            for k in range(SUB):
                j = jt * SUB + k
                rel = jnp.clip(i - j, 0, 2)
                sel = sel_ref[j]
                sk = s[k * LANES:(k + 1) * LANES]
                lg0 = sk[:, :LANES] + bias_ref[p, rel] + sel
                lg1 = sk[:, LANES:] + bias_ref[GROUP_A + p, rel] + sel
                lg_ref[p, j, :, 0:LANES] = lg0
                lg_ref[p, j, :, LANES:2 * LANES] = lg1
                lg = jnp.concatenate([lg0, lg1], axis=1)
                mx = jnp.maximum(mx, jnp.max(lg.reshape(LANES // 8, 8, 2 * LANES), axis=0))
            mx_ref[p] = mx
        return c
    lax.fori_loop(0, n_super, logit_super, 0)

    for p in range(n_pairs):
        m = jnp.max(mx_ref[p], axis=0, keepdims=True)
        mx_ref[p] = jnp.broadcast_to(m, mx_ref.shape[1:])

    def value_super(jt, c):
        vt = vat_ref[0, jt]
        for p in range(n_pairs):
            m = mx_ref[p][0:1]
            parts = []
            ls = ls_ref[p]
            for k in range(SUB):
                pk = jnp.exp(lg_ref[p, jt * SUB + k] - m)
                ls = ls + jnp.sum(pk.reshape(LANES // 8, 8, 2 * LANES), axis=0)
                parts.append(pk.astype(BF16))
            ls_ref[p] = ls
            out_ref[p] += _dot(vt, jnp.concatenate(parts, axis=0))
        return c
    lax.fori_loop(0, n_super, value_super, 0)

    for p in range(n_pairs):
        denom = jnp.sum(ls_ref[p], axis=0, keepdims=True)
        o_ref[:, p * LANES:(p + 1) * LANES] = _pair_transposed_out(out_ref[p] / denom).astype(BF16)


def _tri_prefix_ones():
    r = np.arange(LANES)
    tri = (r[None, :] < r[:, None])
    return jnp.asarray(np.concatenate([tri, np.ones((TL_ROWS - LANES, LANES), bool)], axis=0), dtype=BF16)


def _dsa_prompt_call(pr, kwt, vat, bias3, batch, seq):
    nq = seq // Q_TILE
    n_pairs = N_HEADS_A // 2
    topk = min(TOPK_MAX, seq // 4)
    lt = _tri_prefix_ones()
    qblk = lambda w: pl.BlockSpec((Q_TILE, w), lambda b, i: (b * nq + i, 0))
    kblk = lambda w: pl.BlockSpec((seq, w), lambda b, i: (b, 0))
    return pl.pallas_call(
        functools.partial(_dsa_prompt_kernel, topk=topk),
        grid=(batch, nq),
        in_specs=[qblk(512), pl.BlockSpec((1, LANES, Q_TILE), lambda b, i: (b, 0, i)), kblk(LANES),
                  qblk(512), kblk(LANES), pl.BlockSpec((1,) + vat.shape[1:], lambda b, i: (b, 0, 0, 0)),
                  _const_spec(bias3.shape), _const_spec(lt.shape)],
        out_specs=qblk(512),
        out_shape=jax.ShapeDtypeStruct((batch * seq, 512), BF16),
        scratch_shapes=[
            pltpu.VMEM((N_IDX_HEADS // 2, 2 * Q_TILE, LANES), BF16),
            pltpu.VMEM((n_pairs, 2 * Q_TILE, LANES), BF16),
            pltpu.VMEM((nq, LANES, Q_TILE), F32),
            pltpu.VMEM((nq, LANES, Q_TILE), F32),
            pltpu.VMEM((n_pairs, nq, LANES, 2 * Q_TILE), F32),
            pltpu.VMEM((n_pairs, 8, 2 * Q_TILE), F32),
            pltpu.VMEM((n_pairs, 8, 2 * Q_TILE), F32),
            pltpu.VMEM((n_pairs, LANES, 2 * Q_TILE), F32),
        ],
        compiler_params=pltpu.CompilerParams(dimension_semantics=("arbitrary", "arbitrary"),
                                             vmem_limit_bytes=VMEM_LIMIT),
        name="dsa_prompt",
    )(pr["qi"], kwt, pr["kk"], pr["qa"], pr["kab"], vat, bias3, lt)


def _tri_lower_ones():
    r = np.arange(LANES)
    tri = (r[None, :] > r[:, None])
    top = np.concatenate([tri, tri], axis=1)
    full = np.concatenate([top, np.ones((TL_ROWS - LANES, 2 * LANES), bool)], axis=0)
    return jnp.asarray(-full.astype(np.float32), dtype=BF16)


def _pair_transposed_out(acc):
    top = lax.broadcasted_iota(I32, (LANES, LANES), 0) < HEAD_DIM
    return jnp.where(top, acc[:, :LANES], acc[:, LANES:]).T


def _sb_prompt_kernel(qb_ref, kb_ref, vbt_ref, tl_ref, o_ref, qm_ref, z_ref, hl_ref, cs_ref, a_ref, run_ref, out_ref):
    i = pl.program_id(1)
    n_pairs = N_HEADS_B // 2
    last = i // SUB

    _store_masked_pairs(qb_ref, qm_ref, n_pairs, ATTN_SCALE)
    run_ref[...] = jnp.zeros(run_ref.shape, F32)
    out_ref[...] = jnp.zeros(out_ref.shape, F32)

    key_row = lax.broadcasted_iota(I32, (SUPER, 2 * LANES), 0)
    q_lane = lax.broadcasted_iota(I32, (SUPER, 2 * LANES), 1) % LANES
    strict = key_row < (i % SUB) * LANES + q_lane

    def super_tile(jt, diag):
        rows = pl.ds(pl.multiple_of(jt * SUPER, SUPER), SUPER)
        for p in range(n_pairs):
            z_ref[p] = _dot_nt(kb_ref[rows, p * LANES:(p + 1) * LANES], qm_ref[p])
        for p in range(n_pairs):
            z = z_ref[p]
            sp = _softplus(z)
            hi, lo = _split_bf16(jnp.where(strict, sp, 0.0) if diag else sp)
            for k in range(SUB):
                ks = slice(k * LANES, (k + 1) * LANES)
                hl_ref[p, :, k * 2 * LANES:(k + 1) * 2 * LANES] = jnp.concatenate([hi[ks], lo[ks]], axis=0)
            z_ref[p] = z - sp
        for p in range(n_pairs):
            cs_ref[p] = _dot(tl_ref[...], hl_ref[p])
        for p in range(n_pairs):
            run = run_ref[p]
            for k in reversed(range(SUB)):
                cs = cs_ref[p, :, k * 2 * LANES:(k + 1) * 2 * LANES]
                ks = slice(k * LANES, (k + 1) * LANES)
                a = jnp.exp(z_ref[p, ks, :] + cs[:LANES] + run)
                if diag:
                    a = jnp.where(strict[ks], a, 0.0)
                a_ref[p, ks, :] = a.astype(BF16)
                run = run + cs[LANES:LANES + 1]
            run_ref[p] = run
        for p in range(n_pairs):
            out_ref[p] += _dot(vbt_ref[0, jt, p * LANES:(p + 1) * LANES, :], a_ref[p])

    super_tile(last, True)

    def older(jj, c):
        super_tile(last - 1 - jj, False)
        return c
    lax.fori_loop(0, last, older, 0)

    for p in range(n_pairs):
        o_ref[:, p * LANES:(p + 1) * LANES] = _pair_transposed_out(out_ref[p]).astype(BF16)


def _sb_prompt_call(pr, vbt, tl, batch, seq):
    nq = seq // Q_TILE
    width = N_HEADS_B * HEAD_DIM
    n_pairs = N_HEADS_B // 2
    qblk = pl.BlockSpec((Q_TILE, width), lambda b, i: (b * nq + i, 0))
    kblk = pl.BlockSpec((seq, width), lambda b, i: (b, 0))
    vblk = pl.BlockSpec((1,) + vbt.shape[1:], lambda b, i: (b, 0, 0, 0))
    return pl.pallas_call(
        _sb_prompt_kernel,
        grid=(batch, nq),
        in_specs=[qblk, kblk, vblk, _const_spec(tl.shape)],
        out_specs=qblk,
        out_shape=jax.ShapeDtypeStruct((batch * seq, width), BF16),
        scratch_shapes=[
            pltpu.VMEM((n_pairs, 2 * Q_TILE, LANES), BF16),
            pltpu.VMEM((n_pairs, SUPER, 2 * LANES), F32),
            pltpu.VMEM((n_pairs, 2 * LANES, SUB * 2 * LANES), BF16),
            pltpu.VMEM((n_pairs, TL_ROWS, SUB * 2 * LANES), F32),
            pltpu.VMEM((n_pairs, SUPER, 2 * LANES), BF16),
            pltpu.VMEM((n_pairs, 1, 2 * LANES), F32),
            pltpu.VMEM((n_pairs, LANES, 2 * LANES), F32),
        ],
        compiler_params=pltpu.CompilerParams(dimension_semantics=("arbitrary", "arbitrary"),
                                             vmem_limit_bytes=VMEM_LIMIT),
        name="sb_prompt",
    )(pr["qb"], pr["kbb"], vbt, tl)


SEQ_GROUP = 4
KEY_CHUNK = 1024
SB_CHUNK = 2048


def _page_copy(cache_ref, buf_ref, sem, phys, p, page):
    return pltpu.make_async_copy(cache_ref.at[0, phys], buf_ref.at[:, p * page:(p + 1) * page], sem)


def _start_pages(cache_ref, buf_ref, sem, pt_ref, seq, first_page, n_copy, page):
    for p in range(n_copy):
        _page_copy(cache_ref, buf_ref, sem, pt_ref[seq, first_page + p], p, page).start()


def _wait_pages(cache_ref, buf_ref, sem, n_copy, page):
    for p in range(n_copy):
        _page_copy(cache_ref, buf_ref, sem, 0, p, page).wait()


def _dsa_sample_select_kernel(pt_ref, qi_ref, w_ref, kn_ref, cache_ref, uo_ref, sel_ref,
                              kbuf, sem, key_ref, knew_ref, *, topk, n_pages, page):
    s = pl.program_id(0)
    nstep = pl.num_programs(0)
    slot = s % 2
    past = n_pages * page
    n_tiles = past // LANES + 1
    rows = SEQ_GROUP * 4

    def start(step, sl):
        for g in range(SEQ_GROUP):
            _start_pages(cache_ref, kbuf.at[sl, g], sem.at[sl], pt_ref, step * SEQ_GROUP + g, 0, n_pages, page)

    @pl.when(s == 0)
    def _():
        start(s, slot)

    @pl.when(s + 1 < nstep)
    def _():
        start(s + 1, 1 - slot)

    for g in range(SEQ_GROUP):
        _wait_pages(cache_ref, kbuf.at[slot, g], sem.at[slot], n_pages, page)

    q = qi_ref[...]
    w = w_ref[...] * IDX_HEAD_SCALE
    row = lax.broadcasted_iota(I32, (rows, LANES), 0)
    lane = lax.broadcasted_iota(I32, (rows, LANES), 1)

    def head_sum(sc):
        sc = jnp.maximum(sc * IDX_SCALE, 0.0) * w
        return jnp.sum(sc.reshape(N_IDX_HEADS, rows, sc.shape[-1]), axis=0)

    def own_rows(parts):
        r = lax.broadcasted_iota(I32, parts[0].shape, 0)
        sc = parts[-1]
        for g in reversed(range(SEQ_GROUP - 1)):
            sc = jnp.where(r < 4 * (g + 1), parts[g], sc)
        return sc

    for c in range(past // KEY_CHUNK):
        parts = []
        for g in range(SEQ_GROUP):
            kc = kbuf[slot, g, :, c * KEY_CHUNK:(c + 1) * KEY_CHUNK].astype(BF16)
            parts.append(head_sum(_dot(q, kc)))
        key_ref[:, c * KEY_CHUNK:(c + 1) * KEY_CHUNK] = own_rows(parts)

    parts = []
    for g in range(SEQ_GROUP):
        knew_ref[...] = jnp.zeros(knew_ref.shape, F32)
        knew_ref[0:4, :] = kn_ref[g]
        parts.append(head_sum(_dot_nt(q, knew_ref[...].astype(BF16))))
    sc = own_rows(parts)
    valid_new = jnp.logical_and(lane <= row % 4, lane < 4)
    key_ref[:, past:past + LANES] = jnp.where(valid_new, sc, -jnp.inf)

    key = key_ref[...]
    thr = _search_threshold(lambda cand: jnp.sum(jnp.where(key >= cand, 1, 0), axis=1, keepdims=True), topk, (rows, 1))
    need =(topk - jnp.sum(jnp.where(key > thr, 1, 0), axis=1, keepdims=True)).astype(F32)

    eq = jnp.where(key == thr, 1.0, 0.0)
    stack = jnp.concatenate([eq[:, t * LANES:(t + 1) * LANES] for t in range(n_tiles)], axis=0).astype(BF16)
    cs = _dot(stack, uo_ref[...])
    carry = jnp.zeros((rows, LANES), F32)
    for t in range(n_tiles):
        kt = key[:, t * LANES:(t + 1) * LANES]
        pre = cs[t * rows:(t + 1) * rows, :LANES] + carry
        carry = carry + cs[t * rows:(t + 1) * rows, LANES:]
        take = jnp.logical_or(kt > thr, jnp.logical_and(kt == thr, pre < need))
        if t == n_tiles - 1:
            take = jnp.logical_and(take, valid_new)
        sel_ref[:, t * LANES:(t + 1) * LANES] = jnp.where(take, 0.0, NEG)


def _dsa_sample_select_call(page_table, qi64, wcol, ki_new, cache_k_idx, uo, topk):
    n, n_pages = page_table.shape
    page = cache_k_idx.shape[-1]
    past = n_pages * page
    width = past + LANES
    rows = SEQ_GROUP * 4
    return pl.pallas_call(
        functools.partial(_dsa_sample_select_kernel, topk=topk, n_pages=n_pages, page=page),
        grid_spec=pltpu.PrefetchScalarGridSpec(
            num_scalar_prefetch=1,
            grid=(n // SEQ_GROUP,),
            in_specs=[pl.BlockSpec((N_IDX_HEADS * rows, IDX_DIM), lambda s, pt: (s, 0)),
                      pl.BlockSpec((N_IDX_HEADS * rows, 1), lambda s, pt: (s, 0)),
                      pl.BlockSpec((SEQ_GROUP, 4, IDX_DIM), lambda s, pt: (s, 0, 0)),
                      pl.BlockSpec(memory_space=pl.ANY),
                      pl.BlockSpec(uo.shape, lambda s, pt: (0, 0))],
            out_specs=pl.BlockSpec((rows, width), lambda s, pt: (s, 0)),
            scratch_shapes=[pltpu.VMEM((2, SEQ_GROUP, IDX_DIM, past), F32),
                            pltpu.SemaphoreType.DMA((2,)),
                            pltpu.VMEM((rows, width), F32),
                            pltpu.VMEM((LANES, IDX_DIM), F32)]),
        out_shape=jax.ShapeDtypeStruct((n * 4, width), F32),
        compiler_params=pltpu.CompilerParams(dimension_semantics=("arbitrary",), vmem_limit_bytes=VMEM_LIMIT),
        name="dsa_sample_select",
    )(page_table, qi64, wcol, ki_new, cache_k_idx, uo)


def _dsa_sample_attn_kernel(pt_ref, qa_ref, sel_ref, bias_ref, cb_ref, kn_ref, vn_ref, kcache, vcache, o_ref,
                            kbuf, vbuf, sem, lg_ref, new_ref, *, n_pages, page):
    n = pl.program_id(0)
    nseq = pl.num_programs(0)
    slot = n % 2
    past = n_pages * page
    n_chunks = past // KEY_CHUNK
    rows = N_HEADS_A * 4

    def start(seq, sl):
        _start_pages(kcache, kbuf.at[sl], sem.at[0, sl], pt_ref, seq, 0, n_pages, page)
        _start_pages(vcache, vbuf.at[sl], sem.at[1, sl], pt_ref, seq, 0, n_pages, page)

    @pl.when(n == 0)
    def _():
        start(n, slot)

    @pl.when(n + 1 < nseq)
    def _():
        start(n + 1, 1 - slot)

    lo_half = lax.broadcasted_iota(I32, (rows // 2, LANES), 1) < HEAD_DIM
    q = qa_ref[...].astype(F32)
    q32 = jnp.concatenate([jnp.where(lo_half, q, 0.0), jnp.where(lo_half, 0.0, q)], axis=0).astype(BF16)

    r = lax.broadcasted_iota(I32, (rows, SEQ_GROUP * 4), 0)
    c = lax.broadcasted_iota(I32, (rows, SEQ_GROUP * 4), 1)
    pick = jnp.where(c == (n % SEQ_GROUP) * 4 + (r % 16) // 4, 1.0, 0.0).astype(BF16)
    sel = _dot(pick, sel_ref[...].astype(BF16))

    cb = cb_ref[...]
    bias = bias_ref[...]
    _wait_pages(kcache, kbuf.at[slot], sem.at[0, slot], n_pages, page)
    mx = jnp.full((rows, LANES), NEG, F32)
    for ch in range(n_chunks):
        kc = kbuf[slot, :, ch * KEY_CHUNK:(ch + 1) * KEY_CHUNK].astype(BF16)
        lg = _dot(q32, kc) * ATTN_SCALE + cb + sel[:, ch * KEY_CHUNK:(ch + 1) * KEY_CHUNK]
        if ch == n_chunks - 1:
            near = jnp.concatenate([jnp.zeros((rows, KEY_CHUNK - LANES), F32), bias[:, :LANES] - cb], axis=1)
            lg = lg + near
        lg_ref[ch] = lg
        for t in range(KEY_CHUNK // LANES):
            mx = jnp.maximum(mx, lg[:, t * LANES:(t + 1) * LANES])
    new_ref[...] = jnp.zeros(new_ref.shape, F32)
    new_ref[0:4, :] = kn_ref[0]
    lg_new = _dot_nt(q32, new_ref[...].astype(BF16)) * ATTN_SCALE + bias[:, LANES:] + sel[:, past:]
    m = jnp.max(jnp.maximum(mx, lg_new), axis=1, keepdims=True)

    _wait_pages(vcache, vbuf.at[slot], sem.at[1, slot], n_pages, page)
    new_ref[0:4, :] = vn_ref[0]
    p_new = jnp.exp(lg_new - m)
    out = _dot(p_new.astype(BF16), new_ref[...].astype(BF16))
    lsum = p_new
    for ch in range(n_chunks):
        pr = jnp.exp(lg_ref[ch] - m)
        out = out + _dot_nt(pr.astype(BF16), vbuf[slot, :, ch * KEY_CHUNK:(ch + 1) * KEY_CHUNK].astype(BF16))
        for t in range(KEY_CHUNK // LANES):
            lsum = lsum + pr[:, t * LANES:(t + 1) * LANES]
    out = out / jnp.sum(lsum, axis=1, keepdims=True)
    o_ref[...] = jnp.where(lo_half, out[:rows // 2], out[rows // 2:]).astype(BF16)


def _dsa_sample_attn_call(page_table, qa16, sel, bias32, cb32, ka_new, va_new, cache_k_a, cache_v_a):
    n, n_pages = page_table.shape
    page = cache_k_a.shape[-1]
    past = n_pages * page
    width = past + LANES
    rows = N_HEADS_A * 4
    kv_w = N_KV_A * HEAD_DIM
    return pl.pallas_call(
        functools.partial(_dsa_sample_attn_kernel, n_pages=n_pages, page=page),
        grid_spec=pltpu.PrefetchScalarGridSpec(
            num_scalar_prefetch=1,
            grid=(n,),
            in_specs=[pl.BlockSpec((rows // 2, LANES), lambda s, pt: (s, 0)),
                      pl.BlockSpec((SEQ_GROUP * 4, width), lambda s, pt: (s // SEQ_GROUP, 0)),
                      pl.BlockSpec(bias32.shape, lambda s, pt: (0, 0)),
                      pl.BlockSpec(cb32.shape, lambda s, pt: (0, 0)),
                      pl.BlockSpec((1, 4, kv_w), lambda s, pt: (s, 0, 0)),
                      pl.BlockSpec((1, 4, kv_w), lambda s, pt: (s, 0, 0)),
                      pl.BlockSpec(memory_space=pl.ANY),
                      pl.BlockSpec(memory_space=pl.ANY)],
            out_specs=pl.BlockSpec((rows // 2, LANES), lambda s, pt: (s, 0)),
            scratch_shapes=[pltpu.VMEM((2, kv_w, past), F32),
                            pltpu.VMEM((2, kv_w, past), F32),
                            pltpu.SemaphoreType.DMA((2, 2)),
                            pltpu.VMEM((past // KEY_CHUNK, rows, KEY_CHUNK), F32),
                            pltpu.VMEM((LANES, kv_w), F32)]),
        out_shape=jax.ShapeDtypeStruct((n * rows // 2, LANES), BF16),
        compiler_params=pltpu.CompilerParams(dimension_semantics=("arbitrary",), vmem_limit_bytes=VMEM_LIMIT),
        name="dsa_sample_attn",
    )(page_table, qa16, sel, bias32, cb32, ka_new, va_new, cache_k_a, cache_v_a)


def _sb_sample_kernel(pt_ref, qb_ref, kn_ref, vn_ref, kcache, vcache, to_ref, o_ref,
                      kbuf, vbuf, sem, qbd_ref, acc_ref, oacc_ref, new_ref, *, n_pages, page, chunk):
    n = pl.program_id(0)
    c = pl.program_id(1)
    n_chunks = n_pages * page // chunk
    step = n * n_chunks + c
    slot = step % 2
    pages_per_chunk = chunk // page
    rows = N_HEADS_B * 4
    width = N_HEADS_B * HEAD_DIM
    tiles = chunk // LANES

    def start(st, sl):
        seq = st // n_chunks
        first = n_pages - (st % n_chunks + 1) * pages_per_chunk
        _start_pages(kcache, kbuf.at[sl], sem.at[0, sl], pt_ref, seq, first, pages_per_chunk, page)
        _start_pages(vcache, vbuf.at[sl], sem.at[1, sl], pt_ref, seq, first, pages_per_chunk, page)

    @pl.when(step == 0)
    def _():
        start(step, slot)

    @pl.when(step + 1 < pl.num_programs(0) * n_chunks)
    def _():
        start(step + 1, 1 - slot)

    @pl.when(c == 0)
    def _():
        q = qb_ref[0].astype(F32)
        q32 = jnp.concatenate([jnp.broadcast_to(q[t:t + 1, :], (N_HEADS_B, width)) for t in range(4)], axis=0)
        r = lax.broadcasted_iota(I32, (rows, width), 0)
        l = lax.broadcasted_iota(I32, (rows, width), 1)
        qbd_ref[...] = jnp.where(l // HEAD_DIM == r % N_HEADS_B, q32, 0.0).astype(BF16)
        row = lax.broadcasted_iota(I32, (rows, LANES), 0)
        lane = lax.broadcasted_iota(I32, (rows, LANES), 1)
        mask = jnp.logical_and(lane < row // N_HEADS_B, lane < 4)
        new_ref[...] = jnp.zeros(new_ref.shape, F32)
        new_ref[0:4, :] = kn_ref[0]
        z = _dot_nt(qbd_ref[...], new_ref[...].astype(BF16)) * ATTN_SCALE
        sp = _softplus(z)
        hi, lo = _split_bf16(jnp.where(mask, -sp, 0.0))
        cs = _dot(hi, to_ref[...]) + _dot(lo, to_ref[...])
        a = jnp.where(mask, jnp.exp(z - sp + cs[:, :LANES]), 0.0)
        new_ref[0:4, :] = vn_ref[0]
        oacc_ref[...] = _dot(a.astype(BF16), new_ref[...].astype(BF16))
        acc_ref[...] = cs[:, LANES:]

    _wait_pages(kcache, kbuf.at[slot], sem.at[0, slot], pages_per_chunk, page)
    z = _dot(qbd_ref[...], kbuf[slot].astype(BF16)) * ATTN_SCALE
    sp = _softplus(z)
    stack = jnp.concatenate([-sp[:, t * LANES:(t + 1) * LANES] for t in range(tiles)], axis=0)
    hi, lo = _split_bf16(stack)
    cs = _dot(hi, to_ref[...]) + _dot(lo, to_ref[...])
    run = acc_ref[...]
    parts = [None] * tiles
    for t in reversed(range(tiles)):
        after = cs[t * rows:(t + 1) * rows, :LANES] + run
        run = run + cs[t * rows:(t + 1) * rows, LANES:]
        sl = slice(t * LANES, (t + 1) * LANES)
        parts[t] = jnp.exp(z[:, sl] - sp[:, sl] + after)
    acc_ref[...] = run
    a = jnp.concatenate(parts, axis=1).astype(BF16)
    _wait_pages(vcache, vbuf.at[slot], sem.at[1, slot], pages_per_chunk, page)
    oacc_ref[...] += _dot_nt(a, vbuf[slot].astype(BF16))

    @pl.when(c == n_chunks - 1)
    def _():
        r = lax.broadcasted_iota(I32, (rows, width), 0)
        l = lax.broadcasted_iota(I32, (rows, width), 1)
        diag = jnp.where(l // HEAD_DIM == r % N_HEADS_B, oacc_ref[...], 0.0)
        o_ref[0] = jnp.sum(diag.reshape(4, N_HEADS_B, width), axis=1)


def _sb_sample_call(page_table, qb4, kb_new, vb_new, cache_k_b, cache_v_b, to):
    n, n_pages = page_table.shape
    page = cache_k_b.shape[-1]
    width = N_HEADS_B * HEAD_DIM
    rows = N_HEADS_B * 4
    tok = pl.BlockSpec((1, 4, width), lambda s, c, pt: (s, 0, 0))
    past = n_pages * page
    chunk = SB_CHUNK if past % SB_CHUNK == 0 and past > SB_CHUNK else KEY_CHUNK
    return pl.pallas_call(
        functools.partial(_sb_sample_kernel, n_pages=n_pages, page=page, chunk=chunk),
        grid_spec=pltpu.PrefetchScalarGridSpec(
            num_scalar_prefetch=1,
            grid=(n, past // chunk),
            in_specs=[tok, tok, tok,
                      pl.BlockSpec(memory_space=pl.ANY), pl.BlockSpec(memory_space=pl.ANY),
                      pl.BlockSpec(to.shape, lambda s, c, pt: (0, 0))],
            out_specs=tok,
            scratch_shapes=[pltpu.VMEM((2, width, chunk), F32),
                            pltpu.VMEM((2, width, chunk), F32),
                            pltpu.SemaphoreType.DMA((2, 2)),
                            pltpu.VMEM((rows, width), BF16),
                            pltpu.VMEM((rows, LANES), F32),
                            pltpu.VMEM((rows, width), F32),
                            pltpu.VMEM((LANES, width), F32)]),
        out_shape=jax.ShapeDtypeStruct((n, 4, width), F32),
        compiler_params=pltpu.CompilerParams(dimension_semantics=("arbitrary", "arbitrary"),
                                             vmem_limit_bytes=VMEM_LIMIT),
        name="sb_sample",
    )(page_table, qb4, kb_new, vb_new, cache_k_b, cache_v_b, to)


def _prep_weights(w_in, w_o_a, w_o_b, w_out, w_up1, w_down1, w_up2, w_down2):
    w = w_in[0]
    d_model = w.shape[0]
    width_a = N_HEADS_A * HEAD_DIM
    kv_a = N_KV_A * HEAD_DIM
    width_i = N_IDX_HEADS * IDX_DIM
    width_b = N_HEADS_B * HEAD_DIM
    sizes = (width_a, kv_a, kv_a, width_i, IDX_DIM, N_IDX_HEADS, width_b, width_b, width_b, d_model, d_model)
    offs = np.cumsum((0,) + sizes)
    q_a, k_a, v_a, q_i, k_i, w_i, q_b, k_b, v_b, g_a, g_b = [w[:, int(offs[n]):int(offs[n + 1])] for n in range(len(sizes))]
    perm = np.concatenate([np.arange(HEAD_DIM) + (g * GROUP_A + j) * HEAD_DIM
                           for j in range(GROUP_A) for g in range(N_KV_A)])
    pad = jnp.zeros((d_model, LANES - IDX_DIM - N_IDX_HEADS), w.dtype)
    proj = [q_a[:, perm], k_a, v_a, q_i, jnp.concatenate([k_i, w_i, pad], axis=1),
            jnp.concatenate([k_i, k_i], axis=1), q_b, k_b, v_b, g_a, g_b]
    bf = lambda a: a.astype(BF16)
    kw = proj[4]
    return dict(
        proj=[bf(p) for p in proj],
        proj_rows=[bf(p) for p in (proj[0], k_a, q_i, proj[5], q_b, k_b, g_a, g_b)],
        proj_cols=bf(jnp.concatenate([k_a, v_a, kw, k_b, v_b], axis=1).T),
        woa=bf(w_o_a[0][perm, :]), wob=bf(w_o_b[0]), wout=bf(w_out[0]),
        wup1=bf(w_up1[0]), wdn1=bf(w_down1[0]), wup2=bf(w_up2[0]), wdn2=bf(w_down2[0]),
    )


def _prompt_tables(rel_bias):
    key = np.arange(LANES)[:, None]
    query = np.arange(Q_TILE)[None, :]
    idx = np.concatenate([_t5_bucket_np(rel * LANES + query - key) for rel in range(3)], axis=0)
    return _bias_call(rel_bias, idx).reshape(rel_bias.shape[1], 3, LANES, Q_TILE)


def _prompt_path(xp, wts, rel_bias, g_ffn1, g_mix, g_ffn2, g_final, batch, seq):
    assert seq % SUPER == 0
    x1 = _ffn_call(xp, g_ffn1, wts["wup1"], wts["wdn1"])
    pr = _proj_prompt_call(x1, g_mix, wts["proj_rows"], wts["proj_cols"], batch, seq)
    oa = _dsa_prompt_call(pr, pr["kwt"], pr["vatb"], _prompt_tables(rel_bias), batch, seq)
    ob = _sb_prompt_call(pr, pr["vbtb"], _tri_lower_ones(), batch, seq)
    y = _post_call(x1, oa, ob, pr["sa"], pr["sb"], wts["woa"], wts["wob"], wts["wout"],
                   g_ffn2, wts["wup2"], wts["wdn2"], g_final)
    return y, pr


def _sample_tables(rel_bias):
    t = np.arange(4)[:, None]
    c = np.arange(LANES)[None, :]
    idx = np.concatenate([_t5_bucket_np(LANES + t - c), _t5_bucket_np(t - c)], axis=1)
    bias = _bias_call(rel_bias, idx)
    bias32 = bias.reshape(N_KV_A, GROUP_A, 4, 2 * LANES).transpose(0, 2, 1, 3).reshape(N_HEADS_A * 4, 2 * LANES)
    far = rel_bias[N_BUCKETS - 1].reshape(N_KV_A, 1, GROUP_A, 1)
    cb32 = jnp.broadcast_to(far, (N_KV_A, 4, GROUP_A, 1)).reshape(N_HEADS_A * 4, 1)
    return bias32, cb32


def _sample_path(xs, wts, rel_bias, caches, page_table, g_ffn1, g_mix, g_ffn2, g_final, n_seq, n_tok):
    assert n_tok == 4 and n_seq % SEQ_GROUP == 0
    n_pages = page_table.shape[1]
    page = caches[2].shape[2]

    def key_minor(c):
        nd = c.ndim
        c = jnp.transpose(c, (0, 1) + tuple(range(3, nd)) + (2,))
        return c.reshape(c.shape[:2] + (-1, page))
    cache_k_a, cache_v_a, cache_k_idx, cache_k_b, cache_v_b = [key_minor(c) for c in caches]
    assert (n_pages * page) % KEY_CHUNK == 0 and KEY_CHUNK % page == 0
    x1 = _ffn_call(xs, g_ffn1, wts["wup1"], wts["wdn1"])
    pr = _proj_call(x1, g_mix, wts["proj"])
    topk = min(TOPK_MAX, (n_pages * page + n_tok) // 4)
    half = n_seq // SEQ_GROUP
    qi64 = pr["qi"].reshape(half, SEQ_GROUP, 4, N_IDX_HEADS, IDX_DIM).transpose(0, 3, 1, 2, 4)
    qi64 = qi64.reshape(half * N_IDX_HEADS * SEQ_GROUP * 4, IDX_DIM)
    wcol = pr["kw"][:, IDX_DIM:IDX_DIM + N_IDX_HEADS].reshape(half, SEQ_GROUP, 4, N_IDX_HEADS).transpose(0, 3, 1, 2)
    wcol = wcol.reshape(half * N_IDX_HEADS * SEQ_GROUP * 4, 1)
    ki_new = pr["kw"][:, :IDX_DIM].reshape(n_seq, 4, IDX_DIM)
    sel = _dsa_sample_select_call(page_table, qi64, wcol, ki_new, cache_k_idx, _tri_ones(True), topk)
    bias32, cb32 = _sample_tables(rel_bias)
    oa = _dsa_sample_attn_call(page_table, pr["qa"].reshape(n_seq * 16, LANES), sel, bias32, cb32,
                               pr["ka"].reshape(n_seq, 4, -1), pr["va"].reshape(n_seq, 4, -1), cache_k_a, cache_v_a)
    ob = _sb_sample_call(page_table, pr["qb"].reshape(n_seq, 4, -1), pr["kb"].reshape(n_seq, 4, -1),
                         pr["vb"].reshape(n_seq, 4, -1), cache_k_b, cache_v_b, _tri_ones(False))
    oa = oa.reshape(n_seq * 4, -1)
    ob = ob.reshape(n_seq * 4, -1).astype(BF16)
    y = _post_call(x1, oa, ob, pr["sa"], pr["sb"], wts["woa"], wts["wob"], wts["wout"],
                   g_ffn2, wts["wup2"], wts["wdn2"], g_final)
    return y, pr


def _prompt_rows(pr, batch, seq):
    def heads(a, n):
        return a.reshape(batch, n, HEAD_DIM, seq).transpose(0, 3, 1, 2)[None]
    return (heads(pr["kat"], N_KV_A), heads(pr["vat"], N_KV_A),
            pr["kwt"][:, :IDX_DIM, :].transpose(0, 2, 1)[None],
            heads(pr["kbt"], N_HEADS_B), heads(pr["vbt"], N_HEADS_B))


def _rows(pr, lead):
    depth = (1,)
    return (pr["ka"].reshape(depth + lead + (N_KV_A, HEAD_DIM)),
            pr["va"].reshape(depth + lead + (N_KV_A, HEAD_DIM)),
            pr["kw"][:, :IDX_DIM].reshape(depth + lead + (IDX_DIM,)),
            pr["kb"].reshape(depth + lead + (N_HEADS_B, HEAD_DIM)),
            pr["vb"].reshape(depth + lead + (N_HEADS_B, HEAD_DIM)))


def kernel(x_prompt, x_sample, cache_k_a, cache_v_a, cache_k_idx, cache_k_b, cache_v_b, page_table,
           w_in, w_o_a, w_o_b, w_out, rel_bias, g_ffn1, w_up1, w_down1, g_mix, g_ffn2, w_up2, w_down2, g_final):
    assert w_in.shape[0] == 1, "single-layer step"
    batch, seq, d_model = x_prompt.shape
    n_seq, n_tok, _ = x_sample.shape
    assert seq % Q_TILE == 0
    wts = _prep_weights(w_in, w_o_a, w_o_b, w_out, w_up1, w_down1, w_up2, w_down2)
    gf = g_final.reshape(1, d_model)
    yp, prp = _prompt_path(x_prompt.reshape(batch * seq, d_model), wts, rel_bias,
                           g_ffn1, g_mix, g_ffn2, gf, batch, seq)
    ys, prs = _sample_path(x_sample.reshape(n_seq * n_tok, d_model), wts, rel_bias,
                           (cache_k_a, cache_v_a, cache_k_idx, cache_k_b, cache_v_b), page_table,
                           g_ffn1, g_mix, g_ffn2, gf, n_seq, n_tok)
    return ((yp.reshape(batch, seq, d_model), ys.reshape(n_seq, n_tok, d_model))
            + _prompt_rows(prp, batch, seq) + _rows(prs, (n_seq, n_tok)))
```

```python
import functools
import math

import jax
import jax.numpy as jnp
import numpy as np
from jax import lax
from jax.experimental import pallas as pl
from jax.experimental.pallas import tpu as pltpu

F32 = jnp.float32
BF16 = jnp.bfloat16
I32 = jnp.int32

HEAD_DIM = 64
IDX_DIM = 64
N_HEADS_A = 8
N_KV_A = 2
GROUP_A = N_HEADS_A // N_KV_A
N_IDX_HEADS = 8
N_HEADS_B = 8
TOPK_MAX = 256
N_BUCKETS = 32
MAX_DISTANCE = 128
EPS = 1e-6
ATTN_SCALE = HEAD_DIM ** -0.5
IDX_SCALE = IDX_DIM ** -0.5
IDX_HEAD_SCALE = N_IDX_HEADS ** -0.5

LANES = 128
Q_TILE = 128
NEG = -1e30
INT_MIN = -2 ** 31
VMEM_LIMIT = 56 * 1024 * 1024
FFN_CHUNK = 256
TOKEN_TILE = 512
SUPER = 512
SUB = SUPER // LANES
TL_ROWS = LANES + 16


def _pick_tile(n, pref):
    t = min(n, pref)
    while n % t or t % 8:
        t -= 1
    return t


def _const_spec(shape):
    nd = len(shape)
    return pl.BlockSpec(shape, lambda *_: (0,) * nd, pipeline_mode=pl.Buffered(1))


def _dot(a, b):
    return jnp.dot(a, b, preferred_element_type=F32)


def _dot_nt(a, b):
    return lax.dot_general(a, b, (((1,), (1,)), ((), ())), preferred_element_type=F32)


def _rms(x, g):
    r = lax.rsqrt(jnp.mean(x * x, axis=-1, keepdims=True) + EPS)
    return (x * r) * g


def _softplus(z):
    return jnp.maximum(z, 0.0) + jnp.log(1.0 + jnp.exp(-jnp.abs(z)))


def _split_bf16(x):
    hi = x.astype(BF16)
    lo = (x - hi.astype(F32)).astype(BF16)
    return hi, lo


KEY_NEG_INF = INT_MIN + 0x7FFFFF


def _key_to_float(key):
    key = jnp.maximum(key, KEY_NEG_INF)
    return lax.bitcast_convert_type(key ^ ((key >> 31) & jnp.int32(0x7FFFFFFF)), F32)


def _search_threshold(count_ge, topk, shape):
    def bit_step(it, u):
        bit = lax.shift_left(jnp.int32(1), 31 - it)
        cand = _key_to_float((u | bit) ^ INT_MIN)
        return jnp.where(count_ge(cand) >= topk, u | bit, u)
    u = lax.fori_loop(0, 32, bit_step, jnp.zeros(shape, I32))
    return _key_to_float(u ^ INT_MIN)


def _swiglu(x, g_ref, wup_ref, wdn_ref, act_ref):
    d_ff = wdn_ref.shape[0]
    h = _rms(x, g_ref[...]).astype(BF16)
    for c in range(d_ff // FFN_CHUNK):
        lo = c * FFN_CHUNK
        gate = _dot(h, wup_ref[:, lo:lo + FFN_CHUNK])
        up = _dot(h, wup_ref[:, d_ff + lo:d_ff + lo + FFN_CHUNK])
        act_ref[:, lo:lo + FFN_CHUNK] = (gate * jax.nn.sigmoid(gate) * up).astype(BF16)
    return _dot(act_ref[...], wdn_ref[...])


def _ffn_kernel(x_ref, g_ref, wup_ref, wdn_ref, o_ref, act_ref):
    x = x_ref[...]
    o_ref[...] = x + 0.5 * _swiglu(x, g_ref, wup_ref, wdn_ref, act_ref)


def _ffn_call(x, g, wup, wdn):
    n, d = x.shape
    d_ff = wdn.shape[0]
    tm = _pick_tile(n, TOKEN_TILE)
    row = lambda w: pl.BlockSpec((tm, w), lambda i: (i, 0))
    return pl.pallas_call(
        _ffn_kernel,
        grid=(n // tm,),
        in_specs=[row(d), _const_spec(g.shape), _const_spec(wup.shape), _const_spec(wdn.shape)],
        out_specs=row(d),
        out_shape=jax.ShapeDtypeStruct((n, d), F32),
        scratch_shapes=[pltpu.VMEM((tm, d_ff), BF16)],
        compiler_params=pltpu.CompilerParams(dimension_semantics=("arbitrary",), vmem_limit_bytes=VMEM_LIMIT),
        name="ffn_pre",
    )(x, g, wup, wdn)


def _proj_kernel(x_ref, g_ref, wqa, wka, wva, wqi, wkw, wkk, wqb, wkb, wvb, wga, wgb,
                 qa, ka, va, kab, vab, qi, kw, kk, qb, kb, vb, kbb, vbb, sa, sb):
    h = _rms(x_ref[...], g_ref[...]).astype(BF16)
    qa[...] = _dot(h, wqa[...]).astype(BF16)
    k = _dot(h, wka[...])
    ka[...] = k
    kab[...] = k.astype(BF16)
    v = _dot(h, wva[...])
    va[...] = v
    vab[...] = v.astype(BF16)
    qi[...] = _dot(h, wqi[...]).astype(BF16)
    kw[...] = _dot(h, wkw[...])
    kk[...] = _dot(h, wkk[...]).astype(BF16)
    qb[...] = _dot(h, wqb[...]).astype(BF16)
    k = _dot(h, wkb[...])
    kb[...] = k
    kbb[...] = k.astype(BF16)
    v = _dot(h, wvb[...])
    vb[...] = v
    vbb[...] = v.astype(BF16)
    sa[...] = jax.nn.sigmoid(_dot(h, wga[...]))
    sb[...] = jax.nn.sigmoid(_dot(h, wgb[...]))


_PROJ_OUT = (("qa", 512, BF16), ("ka", 128, F32), ("va", 128, F32), ("kab", 128, BF16), ("vab", 128, BF16),
             ("qi", 512, BF16), ("kw", 128, F32), ("kk", 128, BF16), ("qb", 512, BF16),
             ("kb", 512, F32), ("vb", 512, F32), ("kbb", 512, BF16), ("vbb", 512, BF16),
             ("sa", 1024, F32), ("sb", 1024, F32))


def _proj_call(x, g, weights):
    n, d = x.shape
    tm = _pick_tile(n, TOKEN_TILE)
    row = lambda w: pl.BlockSpec((tm, w), lambda i: (i, 0))
    outs = pl.pallas_call(
        _proj_kernel,
        grid=(n // tm,),
        in_specs=[row(d), _const_spec(g.shape)] + [_const_spec(w.shape) for w in weights],
        out_specs=[row(w) for _, w, _ in _PROJ_OUT],
        out_shape=[jax.ShapeDtypeStruct((n, w), dt) for _, w, dt in _PROJ_OUT],
        compiler_params=pltpu.CompilerParams(dimension_semantics=("arbitrary",), vmem_limit_bytes=VMEM_LIMIT),
        name="proj",
    )(x, g, *weights)
    return dict(zip([nm for nm, _, _ in _PROJ_OUT], outs))


_PROMPT_ROW_OUT = (("qa", 512, BF16), ("kab", 128, BF16), ("qi", 512, BF16), ("kk", 128, BF16),
                   ("qb", 512, BF16), ("kbb", 512, BF16), ("sa", 1024, F32), ("sb", 1024, F32))
_PROMPT_COL_OUT = (("kat", 0, 128), ("vat", 128, 128), ("kwt", 256, 128), ("kbt", 384, 512), ("vbt", 896, 512))


def _proj_prompt_work(x_ref, g_ref, weights, wt_ref, outs):
    wqa, wka, wqi, wkk, wqb, wkb, wga, wgb = weights
    qa, kab, qi, kk, qb, kbb, sa, sb, kat, vat, kwt, kbt, vbt, vatb, vbtb = outs
    h = _rms(x_ref[...], g_ref[...]).astype(BF16)

    def rows_bf16(out, w):
        def item():
            out[...] = _dot(h, w[...]).astype(BF16)
        return item

    def gate(out, w):
        def item():
            out[...] = jax.nn.sigmoid(_dot(h, w[...]))
        return item

    def cols():
        t = _dot_nt(wt_ref[...], h)
        for ref, (_, r0, nr) in zip((kat, vat, kwt, kbt, vbt), _PROMPT_COL_OUT):
            ref[0] = t[r0:r0 + nr]
        vatb[0, 0] = t[128:256].astype(BF16)
        vbtb[0, 0] = t[896:1408].astype(BF16)
    return [rows_bf16(qa, wqa), rows_bf16(kab, wka), rows_bf16(qi, wqi), gate(sa, wga), rows_bf16(kk, wkk),
            rows_bf16(qb, wqb), gate(sb, wgb), rows_bf16(kbb, wkb), cols]


def _proj_prompt_kernel(x_ref, g_ref, *rest):
    for item in _proj_prompt_work(x_ref, g_ref, rest[:8], rest[8], rest[9:]):
        item()


def _proj_prompt_sb_kernel(pt_ref, x_ref, g_ref, *rest, n_pages, page, chunk):
    n_out = len(_PROMPT_ROW_OUT) + len(_PROMPT_COL_OUT) + 2
    sweep_in, outs, ob_ref, sweep_scratch = rest[9:15], rest[15:15 + n_out], rest[15 + n_out], rest[16 + n_out:]
    work = _proj_prompt_work(x_ref, g_ref, rest[:8], rest[8], outs)
    _hosted_sweep(work, pt_ref, sweep_in, ob_ref, sweep_scratch, n_pages, page, chunk)


def _proj_prompt_call(x, g, weights, wt_all, batch, seq, sweep=None):
    n, d = x.shape
    tm = SUPER
    per_b = seq // tm
    ix =(lambda f: (lambda i, pt: f(i))) if sweep is not None else (lambda f: f)
    row = lambda w: pl.BlockSpec((tm, w), ix(lambda i: (i, 0)))
    col = lambda r: pl.BlockSpec((1, r, tm), ix(lambda i: (i // per_b, 0, i % per_b)))
    blk = lambda r: pl.BlockSpec((1, 1, r, tm), ix(lambda i: (i // per_b, i % per_b, 0, 0)))
    const = lambda a: pl.BlockSpec(a.shape, ix(lambda i: (0,) * a.ndim), pipeline_mode=pl.Buffered(1))
    in_specs = [row(d), const(g)] + [const(w) for w in weights] + [const(wt_all)]
    out_specs = ([row(w) for _, w, _ in _PROMPT_ROW_OUT] + [col(nr) for _, _, nr in _PROMPT_COL_OUT]
                 + [blk(128), blk(512)])
    out_shape = ([jax.ShapeDtypeStruct((n, w), dt) for _, w, dt in _PROMPT_ROW_OUT]
                 + [jax.ShapeDtypeStruct((batch, nr, seq), F32) for _, _, nr in _PROMPT_COL_OUT]
                 + [jax.ShapeDtypeStruct((batch, per_b, 128, tm), BF16),
                    jax.ShapeDtypeStruct((batch, per_b, 512, tm), BF16)])
    names = [nm for nm, _, _ in _PROMPT_ROW_OUT] + [nm for nm, _, _ in _PROMPT_COL_OUT] + ["vatb", "vbtb"]
    params = pltpu.CompilerParams(dimension_semantics=("arbitrary",), vmem_limit_bytes=VMEM_LIMIT)
    if sweep is None:
        outs = pl.pallas_call(
            _proj_prompt_kernel, grid=(n // tm,), in_specs=in_specs, out_specs=out_specs, out_shape=out_shape,
            compiler_params=params, name="proj_prompt",
        )(x, g, *weights, wt_all)
        return dict(zip(names, outs)), None
    page_table, qb4, kb_new, vb_new, cache_k_b, cache_v_b, to = sweep
    n_seq, n_pages = page_table.shape
    assert n_seq == n // tm
    chunk = _sweep_chunk_size(page_table, cache_k_b)
    sweep_in, tok, sweep_scratch = _sweep_specs(chunk, to)
    outs = pl.pallas_call(
        functools.partial(_proj_prompt_sb_kernel, n_pages=n_pages, page=cache_k_b.shape[-1], chunk=chunk),
        grid_spec=pltpu.PrefetchScalarGridSpec(
            num_scalar_prefetch=1, grid=(n // tm,), in_specs=in_specs + sweep_in,
            out_specs=out_specs + [tok], scratch_shapes=sweep_scratch),
        out_shape=out_shape + [jax.ShapeDtypeStruct((n_seq, 4, N_HEADS_B * HEAD_DIM), F32)],
        compiler_params=params, name="proj_prompt_sb_sweep",
    )(page_table, x, g, *weights, wt_all, qb4, kb_new, vb_new, cache_k_b, cache_v_b, to)
    return dict(zip(names, outs[:-1])), outs[-1]


def _post_kernel(x_ref, oa_ref, ob_ref, sa_ref, sb_ref, woa, wob, wout, g2_ref, wup_ref, wdn_ref, gf_ref,
                 y_ref, act_ref):
    mix = sa_ref[...] * _dot(oa_ref[...], woa[...]) + sb_ref[...] * _dot(ob_ref[...], wob[...])
    x2 = x_ref[...] + _dot(mix.astype(BF16), wout[...])
    x3 = x2 + 0.5 * _swiglu(x2, g2_ref, wup_ref, wdn_ref, act_ref)
    y_ref[...] = _rms(x3, gf_ref[...])


def _post_call(x, oa, ob, sa, sb, woa, wob, wout, g2, wup, wdn, gf):
    n, d = x.shape
    d_ff = wdn.shape[0]
    tm = _pick_tile(n, TOKEN_TILE)
    row = lambda w: pl.BlockSpec((tm, w), lambda i: (i, 0))
    consts = (woa, wob, wout, g2, wup, wdn, gf)
    return pl.pallas_call(
        _post_kernel,
        grid=(n // tm,),
        in_specs=[row(d), row(oa.shape[1]), row(ob.shape[1]), row(d), row(d)] + [_const_spec(c.shape) for c in consts],
        out_specs=row(d),
        out_shape=jax.ShapeDtypeStruct((n, d), F32),
        scratch_shapes=[pltpu.VMEM((tm, d_ff), BF16)],
        compiler_params=pltpu.CompilerParams(dimension_semantics=("arbitrary",), vmem_limit_bytes=VMEM_LIMIT),
        name="post",
    )(x, oa, ob, sa, sb, *consts)


def _t5_bucket_np(dist):
    max_exact = N_BUCKETS // 2
    d = np.maximum(dist, 0)
    ratio = np.maximum(d, 1).astype(np.float32) / np.float32(max_exact)
    large = max_exact + (np.log(ratio) / np.float32(math.log(MAX_DISTANCE / max_exact))
                         * np.float32(N_BUCKETS - max_exact)).astype(np.int32)
    large = np.minimum(large, N_BUCKETS - 1)
    return np.where(d < max_exact, d, large).astype(np.int32)


def _bias_kernel(rel_ref, idx_ref, out_ref):
    idx = idx_ref[...]
    for h in range(out_ref.shape[0]):
        acc = jnp.zeros(idx.shape, F32)
        for b in range(N_BUCKETS):
            acc = jnp.where(idx == b, rel_ref[b, h], acc)
        out_ref[h] = acc


def _bias_call(rel_bias, idx):
    return pl.pallas_call(
        _bias_kernel,
        in_specs=[pl.BlockSpec(memory_space=pltpu.SMEM), pl.BlockSpec(memory_space=pltpu.VMEM)],
        out_specs=pl.BlockSpec(memory_space=pltpu.VMEM),
        out_shape=jax.ShapeDtypeStruct((rel_bias.shape[1],) + idx.shape, F32),
        name="rel_bias_table",
    )(rel_bias, jnp.asarray(idx))


def _half_masks():
    lane = lax.broadcasted_iota(I32, (Q_TILE, LANES), 1)
    return lane < HEAD_DIM


def _store_masked_pairs(src_ref, dst_ref, n_pairs, scale=1.0):
    lo_half = _half_masks()
    for p in range(n_pairs):
        pair = src_ref[:, p * LANES:(p + 1) * LANES].astype(F32) * scale
        dst_ref[p, 0:Q_TILE, :] = jnp.where(lo_half, pair, 0.0).astype(BF16)
        dst_ref[p, Q_TILE:2 * Q_TILE, :] = jnp.where(lo_half, 0.0, pair).astype(BF16)


def _tri_ones(strict_upper):
    r = np.arange(LANES)
    if strict_upper:
        tri = (r[:, None] < r[None, :])
    else:
        tri = (r[:, None] > r[None, :])
    return jnp.asarray(np.concatenate([tri, np.ones((LANES, LANES), bool)], axis=1), dtype=BF16)


def _dsa_prompt_kernel(qi_ref, kwt_ref, kk_ref, qa_ref, ka_ref, vat_ref, bias_ref, lt_ref, o_ref,
                       qim_ref, qam_ref, key_ref, sel_ref, lg_ref, mx_ref, ls_ref, out_ref, *, topk):
    i = pl.program_id(1)
    ntile = i + 1
    n_super = i // SUB + 1
    n_pairs = N_HEADS_A // 2
    row = lax.broadcasted_iota(I32, (LANES, LANES), 0)
    lane = lax.broadcasted_iota(I32, (LANES, LANES), 1)
    key_minus_query = row - lane

    _store_masked_pairs(qi_ref, qim_ref, N_IDX_HEADS // 2, IDX_SCALE)
    _store_masked_pairs(qa_ref, qam_ref, n_pairs, ATTN_SCALE)
    wt = kwt_ref[0, IDX_DIM:IDX_DIM + N_IDX_HEADS, :] * IDX_HEAD_SCALE
    mx_ref[...] = jnp.full(mx_ref.shape, NEG, F32)
    ls_ref[...] = jnp.zeros(ls_ref.shape, F32)
    out_ref[...] = jnp.zeros(out_ref.shape, F32)

    def super_rows(jt):
        return pl.ds(pl.multiple_of(jt * SUPER, SUPER), SUPER)

    def valid_tile(j):
        return key_minus_query <= (i - j) * LANES

    def score_super(jt, c):
        kt = kk_ref[super_rows(jt), :]
        acc = jnp.zeros((SUPER, LANES), F32)
        for p in range(N_IDX_HEADS // 2):
            s = _dot_nt(kt, qim_ref[p])
            acc = acc + wt[2 * p:2 * p + 1] * jnp.maximum(s[:, :LANES], 0.0)
            acc = acc + wt[2 * p + 1:2 * p + 2] * jnp.maximum(s[:, LANES:], 0.0)
        for k in range(SUB):
            j = jt * SUB + k
            key_ref[j] = jnp.where(valid_tile(j), acc[k * LANES:(k + 1) * LANES], -jnp.inf)
        return c
    lax.fori_loop(0, n_super, score_super, 0)

    def count_tiles(pred):
        def body(jt, cnt):
            for k in range(SUB):
                cnt = cnt + jnp.where(pred(key_ref[jt * SUB + k]), 1, 0)
            return cnt
        cnt = lax.fori_loop(0, n_super, body, jnp.zeros((LANES, LANES), I32))
        return jnp.sum(cnt, axis=0, keepdims=True)

    thr = _search_threshold(lambda cand: count_tiles(lambda k: k >= cand), topk, (1, LANES))
    need = (topk - count_tiles(lambda k: k > thr)).astype(F32)

    over = jnp.logical_and(count_tiles(lambda k: k >= thr) > topk, thr > -jnp.inf)
    ties = jnp.sum(jnp.where(over, 1, 0)) > 0

    @pl.when(ties)
    def _():
        def select_tile(j, carry):
            key = key_ref[j]
            eq = key == thr
            cs = _dot(lt_ref[...], jnp.where(eq, 1.0, 0.0).astype(BF16))
            take = jnp.logical_or(key > thr, jnp.logical_and(eq, cs[:LANES] + carry < need))
            sel_ref[j] = jnp.where(jnp.logical_and(take, valid_tile(j)), 0.0, NEG)
            return carry + cs[LANES:LANES + 1]
        lax.fori_loop(0, n_super * SUB, select_tile, jnp.zeros((1, LANES), F32))

    @pl.when(jnp.logical_not(ties))
    def _():
        def select_tile(j, c):
            take = jnp.logical_and(key_ref[j] >= thr, valid_tile(j))
            sel_ref[j] = jnp.where(take, 0.0, NEG)
            return c
        lax.fori_loop(0, n_super * SUB, select_tile, 0)

    def logit_super(jt, c):
        kt = ka_ref[super_rows(jt), :]
        for p in range(n_pairs):
            s = _dot_nt(kt, qam_ref[p])
            mx = mx_ref[p]
            for k in range(SUB):
                j = jt * SUB + k
                rel = jnp.clip(i - j, 0, 2)
                sel = sel_ref[j]
                sk = s[k * LANES:(k + 1) * LANES]
                lg0 = sk[:, :LANES] + bias_ref[p, rel] + sel
                lg1 = sk[:, LANES:] + bias_ref[GROUP_A + p, rel] + sel
                lg_ref[p, j, :, 0:LANES] = lg0
                lg_ref[p, j, :, LANES:2 * LANES] = lg1
                lg = jnp.concatenate([lg0, lg1], axis=1)
                mx = jnp.maximum(mx, jnp.max(lg.reshape(LANES // 8, 8, 2 * LANES), axis=0))
            mx_ref[p] = mx
        return c
    lax.fori_loop(0, n_super, logit_super, 0)

    for p in range(n_pairs):
        m = jnp.max(mx_ref[p], axis=0, keepdims=True)
        mx_ref[p] = jnp.broadcast_to(m, mx_ref.shape[1:])

    def value_super(jt, c):
        vt = vat_ref[0, jt]
        for p in range(n_pairs):
            m = mx_ref[p][0:1]
            parts = []
            ls = ls_ref[p]
            for k in range(SUB):
                pk = jnp.exp(lg_ref[p, jt * SUB + k] - m)
                ls = ls + jnp.sum(pk.reshape(LANES // 8, 8, 2 * LANES), axis=0)
                parts.append(pk.astype(BF16))
            ls_ref[p] = ls
            out_ref[p] += _dot(vt, jnp.concatenate(parts, axis=0))
        return c
    lax.fori_loop(0, n_super, value_super, 0)

    for p in range(n_pairs):
        denom = jnp.sum(ls_ref[p], axis=0, keepdims=True)
        o_ref[:, p * LANES:(p + 1) * LANES] = _pair_transposed_out(out_ref[p] / denom).astype(BF16)


def _tri_prefix_ones():
    r = np.arange(LANES)
    tri = (r[None, :] < r[:, None])
    return jnp.asarray(np.concatenate([tri, np.ones((TL_ROWS - LANES, LANES), bool)], axis=0), dtype=BF16)


def _dsa_prompt_call(pr, kwt, vat, bias3, batch, seq):
    nq = seq // Q_TILE
    n_pairs = N_HEADS_A // 2
    topk = min(TOPK_MAX, seq // 4)
    lt = _tri_prefix_ones()
    qblk = lambda w: pl.BlockSpec((Q_TILE, w), lambda b, i: (b * nq + i, 0))
    kblk = lambda w: pl.BlockSpec((seq, w), lambda b, i: (b, 0))
    return pl.pallas_call(
        functools.partial(_dsa_prompt_kernel, topk=topk),
        grid=(batch, nq),
        in_specs=[qblk(512), pl.BlockSpec((1, LANES, Q_TILE), lambda b, i: (b, 0, i)), kblk(LANES),
                  qblk(512), kblk(LANES), pl.BlockSpec((1,) + vat.shape[1:], lambda b, i: (b, 0, 0, 0)),
                  _const_spec(bias3.shape), _const_spec(lt.shape)],
        out_specs=qblk(512),
        out_shape=jax.ShapeDtypeStruct((batch * seq, 512), BF16),
        scratch_shapes=[
            pltpu.VMEM((N_IDX_HEADS // 2, 2 * Q_TILE, LANES), BF16),
            pltpu.VMEM((n_pairs, 2 * Q_TILE, LANES), BF16),
            pltpu.VMEM((nq, LANES, Q_TILE), F32),
            pltpu.VMEM((nq, LANES, Q_TILE), F32),
            pltpu.VMEM((n_pairs, nq, LANES, 2 * Q_TILE), F32),
            pltpu.VMEM((n_pairs, 8, 2 * Q_TILE), F32),
            pltpu.VMEM((n_pairs, 8, 2 * Q_TILE), F32),
            pltpu.VMEM((n_pairs, LANES, 2 * Q_TILE), F32),
        ],
        compiler_params=pltpu.CompilerParams(dimension_semantics=("arbitrary", "arbitrary"),
                                             vmem_limit_bytes=VMEM_LIMIT),
        name="dsa_prompt",
    )(pr["qi"], kwt, pr["kk"], pr["qa"], pr["kab"], vat, bias3, lt)


def _tri_lower_ones():
    r = np.arange(LANES)
    tri = (r[None, :] > r[:, None])
    top = np.concatenate([tri, tri], axis=1)
    full = np.concatenate([top, np.ones((TL_ROWS - LANES, 2 * LANES), bool)], axis=0)
    return jnp.asarray(-full.astype(np.float32), dtype=BF16)


def _pair_transposed_out(acc):
    top = lax.broadcasted_iota(I32, (LANES, LANES), 0) < HEAD_DIM
    return jnp.where(top, acc[:, :LANES], acc[:, LANES:]).T


def _sb_prompt_kernel(qb_ref, kb_ref, vbt_ref, tl_ref, o_ref, qm_ref, z_ref, hl_ref, cs_ref, a_ref, run_ref, out_ref):
    i = pl.program_id(1)
    n_pairs = N_HEADS_B // 2
    last = i // SUB

    _store_masked_pairs(qb_ref, qm_ref, n_pairs, ATTN_SCALE)
    run_ref[...] = jnp.zeros(run_ref.shape, F32)
    out_ref[...] = jnp.zeros(out_ref.shape, F32)

    key_row = lax.broadcasted_iota(I32, (SUPER, 2 * LANES), 0)
    q_lane = lax.broadcasted_iota(I32, (SUPER, 2 * LANES), 1) % LANES
    strict = key_row < (i % SUB) * LANES + q_lane

    def super_tile(jt, diag):
        rows = pl.ds(pl.multiple_of(jt * SUPER, SUPER), SUPER)
        for p in range(n_pairs):
            z_ref[p] = _dot_nt(kb_ref[rows, p * LANES:(p + 1) * LANES], qm_ref[p])
        for p in range(n_pairs):
            z = z_ref[p]
            sp = _softplus(z)
            hi, lo = _split_bf16(jnp.where(strict, sp, 0.0) if diag else sp)
            for k in range(SUB):
                ks = slice(k * LANES, (k + 1) * LANES)
                hl_ref[p, :, k * 2 * LANES:(k + 1) * 2 * LANES] = jnp.concatenate([hi[ks], lo[ks]], axis=0)
            z_ref[p] = z - sp
        for p in range(n_pairs):
            cs_ref[p] = _dot(tl_ref[...], hl_ref[p])
        for p in range(n_pairs):
            run = run_ref[p]
            for k in reversed(range(SUB)):
                cs = cs_ref[p, :, k * 2 * LANES:(k + 1) * 2 * LANES]
                ks = slice(k * LANES, (k + 1) * LANES)
                a = jnp.exp(z_ref[p, ks, :] + cs[:LANES] + run)
                if diag:
                    a = jnp.where(strict[ks], a, 0.0)
                a_ref[p, ks, :] = a.astype(BF16)
                run = run + cs[LANES:LANES + 1]
            run_ref[p] = run
        for p in range(n_pairs):
            out_ref[p] += _dot(vbt_ref[0, jt, p * LANES:(p + 1) * LANES, :], a_ref[p])

    super_tile(last, True)

    def older(jj, c):
        super_tile(last - 1 - jj, False)
        return c
    lax.fori_loop(0, last, older, 0)

    for p in range(n_pairs):
        o_ref[:, p * LANES:(p + 1) * LANES] = _pair_transposed_out(out_ref[p]).astype(BF16)


def _sb_prompt_call(pr, vbt, tl, batch, seq):
    nq = seq // Q_TILE
    width = N_HEADS_B * HEAD_DIM
    n_pairs = N_HEADS_B // 2
    qblk = pl.BlockSpec((Q_TILE, width), lambda b, i: (b * nq + i, 0))
    kblk = pl.BlockSpec((seq, width), lambda b, i: (b, 0))
    vblk = pl.BlockSpec((1,) + vbt.shape[1:], lambda b, i: (b, 0, 0, 0))
    return pl.pallas_call(
        _sb_prompt_kernel,
        grid=(batch, nq),
        in_specs=[qblk, kblk, vblk, _const_spec(tl.shape)],
        out_specs=qblk,
        out_shape=jax.ShapeDtypeStruct((batch * seq, width), BF16),
        scratch_shapes=[
            pltpu.VMEM((n_pairs, 2 * Q_TILE, LANES), BF16),
            pltpu.VMEM((n_pairs, SUPER, 2 * LANES), F32),
            pltpu.VMEM((n_pairs, 2 * LANES, SUB * 2 * LANES), BF16),
            pltpu.VMEM((n_pairs, TL_ROWS, SUB * 2 * LANES), F32),
            pltpu.VMEM((n_pairs, SUPER, 2 * LANES), BF16),
            pltpu.VMEM((n_pairs, 1, 2 * LANES), F32),
            pltpu.VMEM((n_pairs, LANES, 2 * LANES), F32),
        ],
        compiler_params=pltpu.CompilerParams(dimension_semantics=("arbitrary", "arbitrary"),
                                             vmem_limit_bytes=VMEM_LIMIT),
        name="sb_prompt",
    )(pr["qb"], pr["kbb"], vbt, tl)


SEQ_GROUP = 4
KEY_CHUNK = 1024
SB_CHUNK = 2048


def _page_copy(cache_ref, buf_ref, sem, phys, p, page):
    return pltpu.make_async_copy(cache_ref.at[0, phys], buf_ref.at[:, p * page:(p + 1) * page], sem)


def _start_pages(cache_ref, buf_ref, sem, pt_ref, seq, first_page, n_copy, page):
    for p in range(n_copy):
        _page_copy(cache_ref, buf_ref, sem, pt_ref[seq, first_page + p], p, page).start()


def _wait_pages(cache_ref, buf_ref, sem, n_copy, page):
    for p in range(n_copy):
        _page_copy(cache_ref, buf_ref, sem, 0, p, page).wait()


def _dsa_sample_select_kernel(pt_ref, qi_ref, w_ref, kn_ref, cache_ref, uo_ref, sel_ref,
                              kbuf, sem, key_ref, knew_ref, *, topk, n_pages, page):
    s = pl.program_id(0)
    nstep = pl.num_programs(0)
    slot = s % 2
    past = n_pages * page
    n_tiles = past // LANES + 1
    rows = SEQ_GROUP * 4

    def start(step, sl):
        for g in range(SEQ_GROUP):
            _start_pages(cache_ref, kbuf.at[sl, g], sem.at[sl], pt_ref, step * SEQ_GROUP + g, 0, n_pages, page)

    @pl.when(s == 0)
    def _():
        start(s, slot)

    @pl.when(s + 1 < nstep)
    def _():
        start(s + 1, 1 - slot)

    for g in range(SEQ_GROUP):
        _wait_pages(cache_ref, kbuf.at[slot, g], sem.at[slot], n_pages, page)

    q = qi_ref[...]
    w = w_ref[...] * IDX_HEAD_SCALE
    row = lax.broadcasted_iota(I32, (rows, LANES), 0)
    lane = lax.broadcasted_iota(I32, (rows, LANES), 1)

    def head_sum(sc):
        sc = jnp.maximum(sc * IDX_SCALE, 0.0) * w
        return jnp.sum(sc.reshape(N_IDX_HEADS, rows, sc.shape[-1]), axis=0)

    def own_rows(parts):
        r = lax.broadcasted_iota(I32, parts[0].shape, 0)
        sc = parts[-1]
        for g in reversed(range(SEQ_GROUP - 1)):
            sc = jnp.where(r < 4 * (g + 1), parts[g], sc)
        return sc

    for c in range(past // KEY_CHUNK):
        parts = []
        for g in range(SEQ_GROUP):
            kc = kbuf[slot, g, :, c * KEY_CHUNK:(c + 1) * KEY_CHUNK].astype(BF16)
            parts.append(head_sum(_dot(q, kc)))
        key_ref[:, c * KEY_CHUNK:(c + 1) * KEY_CHUNK] = own_rows(parts)

    parts = []
    for g in range(SEQ_GROUP):
        knew_ref[...] = jnp.zeros(knew_ref.shape, F32)
        knew_ref[0:4, :] = kn_ref[g]
        parts.append(head_sum(_dot_nt(q, knew_ref[...].astype(BF16))))
    sc = own_rows(parts)
    valid_new = jnp.logical_and(lane <= row % 4, lane < 4)
    key_ref[:, past:past + LANES] = jnp.where(valid_new, sc, -jnp.inf)

    key = key_ref[...]
    thr = _search_threshold(lambda cand: jnp.sum(jnp.where(key >= cand, 1, 0), axis=1, keepdims=True), topk, (rows, 1))
    need =(topk - jnp.sum(jnp.where(key > thr, 1, 0), axis=1, keepdims=True)).astype(F32)

    eq = jnp.where(key == thr, 1.0, 0.0)
    stack = jnp.concatenate([eq[:, t * LANES:(t + 1) * LANES] for t in range(n_tiles)], axis=0).astype(BF16)
    cs = _dot(stack, uo_ref[...])
    carry = jnp.zeros((rows, LANES), F32)
    for t in range(n_tiles):
        kt = key[:, t * LANES:(t + 1) * LANES]
        pre = cs[t * rows:(t + 1) * rows, :LANES] + carry
        carry = carry + cs[t * rows:(t + 1) * rows, LANES:]
        take = jnp.logical_or(kt > thr, jnp.logical_and(kt == thr, pre < need))
        if t == n_tiles - 1:
            take = jnp.logical_and(take, valid_new)
        sel_ref[:, t * LANES:(t + 1) * LANES] = jnp.where(take, 0.0, NEG)


def _dsa_sample_select_call(page_table, qi64, wcol, ki_new, cache_k_idx, uo, topk):
    n, n_pages = page_table.shape
    page = cache_k_idx.shape[-1]
    past = n_pages * page
    width = past + LANES
    rows = SEQ_GROUP * 4
    return pl.pallas_call(
        functools.partial(_dsa_sample_select_kernel, topk=topk, n_pages=n_pages, page=page),
        grid_spec=pltpu.PrefetchScalarGridSpec(
            num_scalar_prefetch=1,
            grid=(n // SEQ_GROUP,),
            in_specs=[pl.BlockSpec((N_IDX_HEADS * rows, IDX_DIM), lambda s, pt: (s, 0)),
                      pl.BlockSpec((N_IDX_HEADS * rows, 1), lambda s, pt: (s, 0)),
                      pl.BlockSpec((SEQ_GROUP, 4, IDX_DIM), lambda s, pt: (s, 0, 0)),
                      pl.BlockSpec(memory_space=pl.ANY),
                      pl.BlockSpec(uo.shape, lambda s, pt: (0, 0))],
            out_specs=pl.BlockSpec((rows, width), lambda s, pt: (s, 0)),
            scratch_shapes=[pltpu.VMEM((2, SEQ_GROUP, IDX_DIM, past), F32),
                            pltpu.SemaphoreType.DMA((2,)),
                            pltpu.VMEM((rows, width), F32),
                            pltpu.VMEM((LANES, IDX_DIM), F32)]),
        out_shape=jax.ShapeDtypeStruct((n * 4, width), F32),
        compiler_params=pltpu.CompilerParams(dimension_semantics=("arbitrary",), vmem_limit_bytes=VMEM_LIMIT),
        name="dsa_sample_select",
    )(page_table, qi64, wcol, ki_new, cache_k_idx, uo)


def _dsa_sample_attn_kernel(pt_ref, qa_ref, sel_ref, bias_ref, cb_ref, kn_ref, vn_ref, kcache, vcache, o_ref,
                            kbuf, vbuf, sem, lg_ref, new_ref, *, n_pages, page):
    n = pl.program_id(0)
    nseq = pl.num_programs(0)
    slot = n % 2
    past = n_pages * page
    n_chunks = past // KEY_CHUNK
    rows = N_HEADS_A * 4

    def start(seq, sl):
        _start_pages(kcache, kbuf.at[sl], sem.at[0, sl], pt_ref, seq, 0, n_pages, page)
        _start_pages(vcache, vbuf.at[sl], sem.at[1, sl], pt_ref, seq, 0, n_pages, page)

    @pl.when(n == 0)
    def _():
        start(n, slot)

    @pl.when(n + 1 < nseq)
    def _():
        start(n + 1, 1 - slot)

    lo_half = lax.broadcasted_iota(I32, (rows // 2, LANES), 1) < HEAD_DIM
    q = qa_ref[...].astype(F32)
    q32 = jnp.concatenate([jnp.where(lo_half, q, 0.0), jnp.where(lo_half, 0.0, q)], axis=0).astype(BF16)

    r = lax.broadcasted_iota(I32, (rows, SEQ_GROUP * 4), 0)
    c = lax.broadcasted_iota(I32, (rows, SEQ_GROUP * 4), 1)
    pick = jnp.where(c == (n % SEQ_GROUP) * 4 + (r % 16) // 4, 1.0, 0.0).astype(BF16)
    sel = _dot(pick, sel_ref[...].astype(BF16))

    cb = cb_ref[...]
    bias = bias_ref[...]
    _wait_pages(kcache, kbuf.at[slot], sem.at[0, slot], n_pages, page)
    mx = jnp.full((rows, LANES), NEG, F32)
    for ch in range(n_chunks):
        kc = kbuf[slot, :, ch * KEY_CHUNK:(ch + 1) * KEY_CHUNK].astype(BF16)
        lg = _dot(q32, kc) * ATTN_SCALE + cb + sel[:, ch * KEY_CHUNK:(ch + 1) * KEY_CHUNK]
        if ch == n_chunks - 1:
            near = jnp.concatenate([jnp.zeros((rows, KEY_CHUNK - LANES), F32), bias[:, :LANES] - cb], axis=1)
            lg = lg + near
        lg_ref[ch] = lg
        for t in range(KEY_CHUNK // LANES):
            mx = jnp.maximum(mx, lg[:, t * LANES:(t + 1) * LANES])
    new_ref[...] = jnp.zeros(new_ref.shape, F32)
    new_ref[0:4, :] = kn_ref[0]
    lg_new = _dot_nt(q32, new_ref[...].astype(BF16)) * ATTN_SCALE + bias[:, LANES:] + sel[:, past:]
    m = jnp.max(jnp.maximum(mx, lg_new), axis=1, keepdims=True)

    _wait_pages(vcache, vbuf.at[slot], sem.at[1, slot], n_pages, page)
    new_ref[0:4, :] = vn_ref[0]
    p_new = jnp.exp(lg_new - m)
    out = _dot(p_new.astype(BF16), new_ref[...].astype(BF16))
    lsum = p_new
    for ch in range(n_chunks):
        pr = jnp.exp(lg_ref[ch] - m)
        out = out + _dot_nt(pr.astype(BF16), vbuf[slot, :, ch * KEY_CHUNK:(ch + 1) * KEY_CHUNK].astype(BF16))
        for t in range(KEY_CHUNK // LANES):
            lsum = lsum + pr[:, t * LANES:(t + 1) * LANES]
    out = out / jnp.sum(lsum, axis=1, keepdims=True)
    o_ref[...] = jnp.where(lo_half, out[:rows // 2], out[rows // 2:]).astype(BF16)


def _dsa_sample_attn_call(page_table, qa16, sel, bias32, cb32, ka_new, va_new, cache_k_a, cache_v_a):
    n, n_pages = page_table.shape
    page = cache_k_a.shape[-1]
    past = n_pages * page
    width = past + LANES
    rows = N_HEADS_A * 4
    kv_w = N_KV_A * HEAD_DIM
    return pl.pallas_call(
        functools.partial(_dsa_sample_attn_kernel, n_pages=n_pages, page=page),
        grid_spec=pltpu.PrefetchScalarGridSpec(
            num_scalar_prefetch=1,
            grid=(n,),
            in_specs=[pl.BlockSpec((rows // 2, LANES), lambda s, pt: (s, 0)),
                      pl.BlockSpec((SEQ_GROUP * 4, width), lambda s, pt: (s // SEQ_GROUP, 0)),
                      pl.BlockSpec(bias32.shape, lambda s, pt: (0, 0)),
                      pl.BlockSpec(cb32.shape, lambda s, pt: (0, 0)),
                      pl.BlockSpec((1, 4, kv_w), lambda s, pt: (s, 0, 0)),
                      pl.BlockSpec((1, 4, kv_w), lambda s, pt: (s, 0, 0)),
                      pl.BlockSpec(memory_space=pl.ANY),
                      pl.BlockSpec(memory_space=pl.ANY)],
            out_specs=pl.BlockSpec((rows // 2, LANES), lambda s, pt: (s, 0)),
            scratch_shapes=[pltpu.VMEM((2, kv_w, past), F32),
                            pltpu.VMEM((2, kv_w, past), F32),
                            pltpu.SemaphoreType.DMA((2, 2)),
                            pltpu.VMEM((past // KEY_CHUNK, rows, KEY_CHUNK), F32),
                            pltpu.VMEM((LANES, kv_w), F32)]),
        out_shape=jax.ShapeDtypeStruct((n * rows // 2, LANES), BF16),
        compiler_params=pltpu.CompilerParams(dimension_semantics=("arbitrary",), vmem_limit_bytes=VMEM_LIMIT),
        name="dsa_sample_attn",
    )(page_table, qa16, sel, bias32, cb32, ka_new, va_new, cache_k_a, cache_v_a)


def _sb_new_tokens(qb_ref, kn_ref, vn_ref, to_ref, qbd_ref, acc_ref, oacc_ref, new_ref):
    rows, width = qbd_ref.shape
    q = qb_ref[0].astype(F32)
    q32 = jnp.concatenate([jnp.broadcast_to(q[t:t + 1, :], (N_HEADS_B, width)) for t in range(4)], axis=0)
    r = lax.broadcasted_iota(I32, (rows, width), 0)
    l = lax.broadcasted_iota(I32, (rows, width), 1)
    qbd_ref[...] = jnp.where(l // HEAD_DIM == r % N_HEADS_B, q32, 0.0).astype(BF16)
    row = lax.broadcasted_iota(I32, (rows, LANES), 0)
    lane = lax.broadcasted_iota(I32, (rows, LANES), 1)
    mask = jnp.logical_and(lane < row // N_HEADS_B, lane < 4)
    new_ref[...] = jnp.zeros(new_ref.shape, F32)
    new_ref[0:4, :] = kn_ref[0]
    z = _dot_nt(qbd_ref[...], new_ref[...].astype(BF16)) * ATTN_SCALE
    sp = _softplus(z)
    hi, lo = _split_bf16(jnp.where(mask, -sp, 0.0))
    cs = _dot(hi, to_ref[...]) + _dot(lo, to_ref[...])
    a = jnp.where(mask, jnp.exp(z - sp + cs[:, :LANES]), 0.0)
    new_ref[0:4, :] = vn_ref[0]
    oacc_ref[...] = _dot(a.astype(BF16), new_ref[...].astype(BF16))
    acc_ref[...] = cs[:, LANES:]


def _sb_chunk(to_ref, qbd_ref, acc_ref, oacc_ref, k_ref, v_ref, wait_k, wait_v):
    rows = qbd_ref.shape[0]
    tiles = k_ref.shape[-1] // LANES
    wait_k()
    z = _dot(qbd_ref[...], k_ref[...].astype(BF16)) * ATTN_SCALE
    sp = _softplus(z)
    stack = jnp.concatenate([-sp[:, t * LANES:(t + 1) * LANES] for t in range(tiles)], axis=0)
    hi, lo = _split_bf16(stack)
    cs = _dot(hi, to_ref[...]) + _dot(lo, to_ref[...])
    run = acc_ref[...]
    parts = [None] * tiles
    for t in reversed(range(tiles)):
        after = cs[t * rows:(t + 1) * rows, :LANES] + run
        run = run + cs[t * rows:(t + 1) * rows, LANES:]
        sl = slice(t * LANES, (t + 1) * LANES)
        parts[t] = jnp.exp(z[:, sl] - sp[:, sl] + after)
    acc_ref[...] = run
    a = jnp.concatenate(parts, axis=1).astype(BF16)
    wait_v()
    oacc_ref[...] += _dot_nt(a, v_ref[...].astype(BF16))


def _sb_finish(oacc_ref, o_ref):
    rows, width = oacc_ref.shape
    r = lax.broadcasted_iota(I32, (rows, width), 0)
    l = lax.broadcasted_iota(I32, (rows, width), 1)
    diag = jnp.where(l // HEAD_DIM == r % N_HEADS_B, oacc_ref[...], 0.0)
    o_ref[0] = jnp.sum(diag.reshape(4, N_HEADS_B, width), axis=1)


def _sb_sample_kernel(pt_ref, qb_ref, kn_ref, vn_ref, kcache, vcache, to_ref, o_ref,
                      kbuf, vbuf, sem, qbd_ref, acc_ref, oacc_ref, new_ref, *, n_pages, page, chunk):
    n = pl.program_id(0)
    c = pl.program_id(1)
    n_chunks = n_pages * page // chunk
    step = n * n_chunks + c
    slot = step % 2
    pages_per_chunk = chunk // page
    rows = N_HEADS_B * 4
    width = N_HEADS_B * HEAD_DIM
    tiles = chunk // LANES

    def start(st, sl):
        seq = st // n_chunks
        first = n_pages - (st % n_chunks + 1) * pages_per_chunk
        _start_pages(kcache, kbuf.at[sl], sem.at[0, sl], pt_ref, seq, first, pages_per_chunk, page)
        _start_pages(vcache, vbuf.at[sl], sem.at[1, sl], pt_ref, seq, first, pages_per_chunk, page)

    @pl.when(step == 0)
    def _():
        start(step, slot)

    @pl.when(step + 1 < pl.num_programs(0) * n_chunks)
    def _():
        start(step + 1, 1 - slot)

    @pl.when(c == 0)
    def _():
        _sb_new_tokens(qb_ref, kn_ref, vn_ref, to_ref, qbd_ref, acc_ref, oacc_ref, new_ref)

    _sb_chunk(to_ref, qbd_ref, acc_ref, oacc_ref, kbuf.at[slot], vbuf.at[slot],
              lambda: _wait_pages(kcache, kbuf.at[slot], sem.at[0, slot], pages_per_chunk, page),
              lambda: _wait_pages(vcache, vbuf.at[slot], sem.at[1, slot], pages_per_chunk, page))

    @pl.when(c == n_chunks - 1)
    def _():
        _sb_finish(oacc_ref, o_ref)


def _hosted_sweep(work, pt_ref, sweep_in, ob_ref, sweep_scratch, n_pages, page, chunk):
    qb_ref, kn_ref, vn_ref, kcache, vcache, to_ref = sweep_in
    kbuf, vbuf, sem, qbd_ref, acc_ref, oacc_ref, new_ref = sweep_scratch
    s = pl.program_id(0)
    last = pl.num_programs(0) - 1
    n_chunks = n_pages * page // chunk
    pages_per_chunk = chunk // page

    def start(seq, c):
        first = n_pages - (c + 1) * pages_per_chunk
        _start_pages(kcache, kbuf.at[c % 2], sem.at[0, c % 2], pt_ref, seq, first, pages_per_chunk, page)
        _start_pages(vcache, vbuf.at[c % 2], sem.at[1, c % 2], pt_ref, seq, first, pages_per_chunk, page)

    def wait(c, which):
        cache, buf = ((kcache, kbuf), (vcache, vbuf))[which]
        _wait_pages(cache, buf.at[c % 2], sem.at[which, c % 2], pages_per_chunk, page)

    def sweep_chunk(c):
        wait(c, 0)
        wait(c, 1)
        if c + 1 < n_chunks:
            start(s, c + 1)
        else:
            start(jnp.minimum(s + 1, last), 0)
        if c == 0:
            _sb_new_tokens(qb_ref, kn_ref, vn_ref, to_ref, qbd_ref, acc_ref, oacc_ref, new_ref)
        _sb_chunk(to_ref, qbd_ref, acc_ref, oacc_ref, kbuf.at[c % 2], vbuf.at[c % 2], lambda: None, lambda: None)
        if c == n_chunks - 1:
            _sb_finish(oacc_ref, ob_ref)

    @pl.when(s == 0)
    def _():
        start(s, 0)

    sweep_at = {(k * len(work)) // n_chunks: k for k in range(n_chunks)}
    for i, item in enumerate(work):
        if i in sweep_at:
            sweep_chunk(sweep_at[i])
        item()

    @pl.when(s == last)
    def _():
        wait(0, 0)
        wait(0, 1)


def _sweep_chunk_size(page_table, cache):
    past = page_table.shape[1] * cache.shape[-1]
    return SB_CHUNK if past % SB_CHUNK == 0 and past > SB_CHUNK else KEY_CHUNK


def _sweep_specs(chunk, to):
    width = N_HEADS_B * HEAD_DIM
    rows = N_HEADS_B * 4
    tok = pl.BlockSpec((1, 4, width), lambda i, pt: (i, 0, 0))
    in_specs = [tok, tok, tok, pl.BlockSpec(memory_space=pl.ANY), pl.BlockSpec(memory_space=pl.ANY),
                pl.BlockSpec(to.shape, lambda i, pt: (0, 0), pipeline_mode=pl.Buffered(1))]
    scratch = [pltpu.VMEM((2, width, chunk), F32), pltpu.VMEM((2, width, chunk), F32),
               pltpu.SemaphoreType.DMA((2, 2)), pltpu.VMEM((rows, width), BF16), pltpu.VMEM((rows, LANES), F32),
               pltpu.VMEM((rows, width), F32), pltpu.VMEM((LANES, width), F32)]
    return in_specs, tok, scratch


def _ffn_sb_kernel(pt_ref, x_ref, g_ref, wup_ref, wdn_ref, *rest, n_pages, page, chunk):
    sweep_in, (o_ref, ob_ref, act_ref), sweep_scratch = rest[:6], rest[6:9], rest[9:]
    d_ff = wdn_ref.shape[0]
    x = x_ref[...]
    h = _rms(x, g_ref[...]).astype(BF16)

    def up_chunk(c):
        def item():
            lo = c * FFN_CHUNK
            gate = _dot(h, wup_ref[:, lo:lo + FFN_CHUNK])
            up = _dot(h, wup_ref[:, d_ff + lo:d_ff + lo + FFN_CHUNK])
            act_ref[:, lo:lo + FFN_CHUNK] = (gate * jax.nn.sigmoid(gate) * up).astype(BF16)
        return item

    def down():
        o_ref[...] = x + 0.5 * _dot(act_ref[...], wdn_ref[...])
    work = [up_chunk(c) for c in range(d_ff // FFN_CHUNK)] + [down]
    _hosted_sweep(work, pt_ref, sweep_in, ob_ref, sweep_scratch, n_pages, page, chunk)


def _ffn_sb_call(x, g, wup, wdn, page_table, qb4, kb_new, vb_new, cache_k_b, cache_v_b, to):
    n, d = x.shape
    d_ff = wdn.shape[0]
    tm = _pick_tile(n, TOKEN_TILE)
    n_seq, n_pages = page_table.shape
    assert n_seq == n // tm
    chunk = _sweep_chunk_size(page_table, cache_k_b)
    sweep_in, tok, sweep_scratch = _sweep_specs(chunk, to)
    row = lambda w: pl.BlockSpec((tm, w), lambda i, pt: (i, 0))
    const = lambda a: pl.BlockSpec(a.shape, lambda i, pt: (0,) * a.ndim, pipeline_mode=pl.Buffered(1))
    return pl.pallas_call(
        functools.partial(_ffn_sb_kernel, n_pages=n_pages, page=cache_k_b.shape[-1], chunk=chunk),
        grid_spec=pltpu.PrefetchScalarGridSpec(
            num_scalar_prefetch=1,
            grid=(n // tm,),
            in_specs=[row(d), const(g), const(wup), const(wdn)] + sweep_in,
            out_specs=[row(d), tok],
            scratch_shapes=[pltpu.VMEM((tm, d_ff), BF16)] + sweep_scratch),
        out_shape=[jax.ShapeDtypeStruct((n, d), F32), jax.ShapeDtypeStruct((n_seq, 4, N_HEADS_B * HEAD_DIM), F32)],
        compiler_params=pltpu.CompilerParams(dimension_semantics=("arbitrary",), vmem_limit_bytes=VMEM_LIMIT),
        name="ffn_pre_sb_sweep",
    )(page_table, x, g, wup, wdn, qb4, kb_new, vb_new, cache_k_b, cache_v_b, to)


def _sb_sample_call(page_table, qb4, kb_new, vb_new, cache_k_b, cache_v_b, to):
    n, n_pages = page_table.shape
    page = cache_k_b.shape[-1]
    width = N_HEADS_B * HEAD_DIM
    rows = N_HEADS_B * 4
    tok = pl.BlockSpec((1, 4, width), lambda s, c, pt: (s, 0, 0))
    past = n_pages * page
    chunk = SB_CHUNK if past % SB_CHUNK == 0 and past > SB_CHUNK else KEY_CHUNK
    return pl.pallas_call(
        functools.partial(_sb_sample_kernel, n_pages=n_pages, page=page, chunk=chunk),
        grid_spec=pltpu.PrefetchScalarGridSpec(
            num_scalar_prefetch=1,
            grid=(n, past // chunk),
            in_specs=[tok, tok, tok,
                      pl.BlockSpec(memory_space=pl.ANY), pl.BlockSpec(memory_space=pl.ANY),
                      pl.BlockSpec(to.shape, lambda s, c, pt: (0, 0))],
            out_specs=tok,
            scratch_shapes=[pltpu.VMEM((2, width, chunk), F32),
                            pltpu.VMEM((2, width, chunk), F32),
                            pltpu.SemaphoreType.DMA((2, 2)),
                            pltpu.VMEM((rows, width), BF16),
                            pltpu.VMEM((rows, LANES), F32),
                            pltpu.VMEM((rows, width), F32),
                            pltpu.VMEM((LANES, width), F32)]),
        out_shape=jax.ShapeDtypeStruct((n, 4, width), F32),
        compiler_params=pltpu.CompilerParams(dimension_semantics=("arbitrary", "arbitrary"),
                                             vmem_limit_bytes=VMEM_LIMIT),
        name="sb_sample",
    )(page_table, qb4, kb_new, vb_new, cache_k_b, cache_v_b, to)


def _prep_weights(w_in, w_o_a, w_o_b, w_out, w_up1, w_down1, w_up2, w_down2):
    w = w_in[0]
    d_model = w.shape[0]
    width_a = N_HEADS_A * HEAD_DIM
    kv_a = N_KV_A * HEAD_DIM
    width_i = N_IDX_HEADS * IDX_DIM
    width_b = N_HEADS_B * HEAD_DIM
    sizes = (width_a, kv_a, kv_a, width_i, IDX_DIM, N_IDX_HEADS, width_b, width_b, width_b, d_model, d_model)
    offs = np.cumsum((0,) + sizes)
    q_a, k_a, v_a, q_i, k_i, w_i, q_b, k_b, v_b, g_a, g_b = [w[:, int(offs[n]):int(offs[n + 1])] for n in range(len(sizes))]
    perm = np.concatenate([np.arange(HEAD_DIM) + (g * GROUP_A + j) * HEAD_DIM
                           for j in range(GROUP_A) for g in range(N_KV_A)])
    pad = jnp.zeros((d_model, LANES - IDX_DIM - N_IDX_HEADS), w.dtype)
    proj = [q_a[:, perm], k_a, v_a, q_i, jnp.concatenate([k_i, w_i, pad], axis=1),
            jnp.concatenate([k_i, k_i], axis=1), q_b, k_b, v_b, g_a, g_b]
    bf = lambda a: a.astype(BF16)
    kw = proj[4]
    return dict(
        proj=[bf(p) for p in proj],
        proj_rows=[bf(p) for p in (proj[0], k_a, q_i, proj[5], q_b, k_b, g_a, g_b)],
        proj_cols=bf(jnp.concatenate([k_a, v_a, kw, k_b, v_b], axis=1).T),
        woa=bf(w_o_a[0][perm, :]), wob=bf(w_o_b[0]), wout=bf(w_out[0]),
        wup1=bf(w_up1[0]), wdn1=bf(w_down1[0]), wup2=bf(w_up2[0]), wdn2=bf(w_down2[0]),
    )


def _prompt_tables(rel_bias):
    key = np.arange(LANES)[:, None]
    query = np.arange(Q_TILE)[None, :]
    idx = np.concatenate([_t5_bucket_np(rel * LANES + query - key) for rel in range(3)], axis=0)
    return _bias_call(rel_bias, idx).reshape(rel_bias.shape[1], 3, LANES, Q_TILE)


def _prompt_path(xp, wts, rel_bias, g_ffn1, g_mix, g_ffn2, g_final, batch, seq, sweep):
    assert seq % SUPER == 0
    n_host = xp.shape[0] // SUPER
    assert _pick_tile(xp.shape[0], TOKEN_TILE) == SUPER
    page_table, qb4, kb_new, vb_new, cache_k_b, cache_v_b = sweep
    n_chunks = page_table.shape[1] * cache_k_b.shape[-1] // _sweep_chunk_size(page_table, cache_k_b)
    to = _tri_ones(False)

    def hosted(k):
        if (k + 1) * n_host > page_table.shape[0] or n_chunks % 2:
            return None
        sl = slice(k * n_host, (k + 1) * n_host)
        return (page_table[sl], qb4[sl], kb_new[sl], vb_new[sl], cache_k_b, cache_v_b, to)

    heads = []
    if hosted(0) is not None:
        x1, ob0 = _ffn_sb_call(xp, g_ffn1, wts["wup1"], wts["wdn1"], *hosted(0))
        heads.append(ob0)
    else:
        x1 = _ffn_call(xp, g_ffn1, wts["wup1"], wts["wdn1"])
    pr, ob1 = _proj_prompt_call(x1, g_mix, wts["proj_rows"], wts["proj_cols"], batch, seq,
                                hosted(1) if heads else None)
    if ob1 is not None:
        heads.append(ob1)
    oa = _dsa_prompt_call(pr, pr["kwt"], pr["vatb"], _prompt_tables(rel_bias), batch, seq)
    ob = _sb_prompt_call(pr, pr["vbtb"], _tri_lower_ones(), batch, seq)
    y = _post_call(x1, oa, ob, pr["sa"], pr["sb"], wts["woa"], wts["wob"], wts["wout"],
                   g_ffn2, wts["wup2"], wts["wdn2"], g_final)
    ob_head = jnp.concatenate(heads, axis=0) if heads else None
    return y, pr, ob_head


def _sample_tables(rel_bias):
    t = np.arange(4)[:, None]
    c = np.arange(LANES)[None, :]
    idx = np.concatenate([_t5_bucket_np(LANES + t - c), _t5_bucket_np(t - c)], axis=1)
    bias = _bias_call(rel_bias, idx)
    bias32 = bias.reshape(N_KV_A, GROUP_A, 4, 2 * LANES).transpose(0, 2, 1, 3).reshape(N_HEADS_A * 4, 2 * LANES)
    far = rel_bias[N_BUCKETS - 1].reshape(N_KV_A, 1, GROUP_A, 1)
    cb32 = jnp.broadcast_to(far, (N_KV_A, 4, GROUP_A, 1)).reshape(N_HEADS_A * 4, 1)
    return bias32, cb32


def _key_minor(c):
    nd = c.ndim
    page = c.shape[2]
    c = jnp.transpose(c, (0, 1) + tuple(range(3, nd)) + (2,))
    return c.reshape(c.shape[:2] + (-1, page))


def _sample_proj(xs, wts, g_ffn1, g_mix):
    x1 = _ffn_call(xs, g_ffn1, wts["wup1"], wts["wdn1"])
    return x1, _proj_call(x1, g_mix, wts["proj"])


def _sample_path(x1, pr, ob_head, wts, rel_bias, caches, page_table, g_ffn2, g_final, n_seq, n_tok):
    assert n_tok == 4 and n_seq % SEQ_GROUP == 0
    n_pages = page_table.shape[1]
    cache_k_a, cache_v_a, cache_k_idx, cache_k_b, cache_v_b = caches
    page = cache_k_idx.shape[-1]
    assert (n_pages * page) % KEY_CHUNK == 0 and KEY_CHUNK % page == 0
    topk = min(TOPK_MAX, (n_pages * page + n_tok) // 4)
    half = n_seq // SEQ_GROUP
    qi64 = pr["qi"].reshape(half, SEQ_GROUP, 4, N_IDX_HEADS, IDX_DIM).transpose(0, 3, 1, 2, 4)
    qi64 = qi64.reshape(half * N_IDX_HEADS * SEQ_GROUP * 4, IDX_DIM)
    wcol = pr["kw"][:, IDX_DIM:IDX_DIM + N_IDX_HEADS].reshape(half, SEQ_GROUP, 4, N_IDX_HEADS).transpose(0, 3, 1, 2)
    wcol = wcol.reshape(half * N_IDX_HEADS * SEQ_GROUP * 4, 1)
    ki_new = pr["kw"][:, :IDX_DIM].reshape(n_seq, 4, IDX_DIM)
    sel = _dsa_sample_select_call(page_table, qi64, wcol, ki_new, cache_k_idx, _tri_ones(True), topk)
    bias32, cb32 = _sample_tables(rel_bias)
    oa = _dsa_sample_attn_call(page_table, pr["qa"].reshape(n_seq * 16, LANES), sel, bias32, cb32,
                               pr["ka"].reshape(n_seq, 4, -1), pr["va"].reshape(n_seq, 4, -1), cache_k_a, cache_v_a)
    done = 0 if ob_head is None else ob_head.shape[0]
    parts = [] if ob_head is None else [ob_head]
    if done < n_seq:
        parts.append(_sb_sample_call(page_table[done:], pr["qb"].reshape(n_seq, 4, -1)[done:],
                                     pr["kb"].reshape(n_seq, 4, -1)[done:], pr["vb"].reshape(n_seq, 4, -1)[done:],
                                     cache_k_b, cache_v_b, _tri_ones(False)))
    ob = jnp.concatenate(parts, axis=0)
    oa = oa.reshape(n_seq * 4, -1)
    ob = ob.reshape(n_seq * 4, -1).astype(BF16)
    y = _post_call(x1, oa, ob, pr["sa"], pr["sb"], wts["woa"], wts["wob"], wts["wout"],
                   g_ffn2, wts["wup2"], wts["wdn2"], g_final)
    return y, pr


def _prompt_rows(pr, batch, seq):
    def heads(a, n):
        return a.reshape(batch, n, HEAD_DIM, seq).transpose(0, 3, 1, 2)[None]
    return (heads(pr["kat"], N_KV_A), heads(pr["vat"], N_KV_A),
            pr["kwt"][:, :IDX_DIM, :].transpose(0, 2, 1)[None],
            heads(pr["kbt"], N_HEADS_B), heads(pr["vbt"], N_HEADS_B))


def _rows(pr, lead):
    depth = (1,)
    return (pr["ka"].reshape(depth + lead + (N_KV_A, HEAD_DIM)),
            pr["va"].reshape(depth + lead + (N_KV_A, HEAD_DIM)),
            pr["kw"][:, :IDX_DIM].reshape(depth + lead + (IDX_DIM,)),
            pr["kb"].reshape(depth + lead + (N_HEADS_B, HEAD_DIM)),
            pr["vb"].reshape(depth + lead + (N_HEADS_B, HEAD_DIM)))


def kernel(x_prompt, x_sample, cache_k_a, cache_v_a, cache_k_idx, cache_k_b, cache_v_b, page_table,
           w_in, w_o_a, w_o_b, w_out, rel_bias, g_ffn1, w_up1, w_down1, g_mix, g_ffn2, w_up2, w_down2, g_final):
    assert w_in.shape[0] == 1, "single-layer step"
    batch, seq, d_model = x_prompt.shape
    n_seq, n_tok, _ = x_sample.shape
    assert seq % Q_TILE == 0
    wts = _prep_weights(w_in, w_o_a, w_o_b, w_out, w_up1, w_down1, w_up2, w_down2)
    gf = g_final.reshape(1, d_model)
    caches = [_key_minor(c) for c in (cache_k_a, cache_v_a, cache_k_idx, cache_k_b, cache_v_b)]
    x1s, prs = _sample_proj(x_sample.reshape(n_seq * n_tok, d_model), wts, g_ffn1, g_mix)
    sweep = (page_table, prs["qb"].reshape(n_seq, n_tok, -1), prs["kb"].reshape(n_seq, n_tok, -1),
             prs["vb"].reshape(n_seq, n_tok, -1), caches[3], caches[4])
    yp, prp, ob_head = _prompt_path(x_prompt.reshape(batch * seq, d_model), wts, rel_bias,
                                    g_ffn1, g_mix, g_ffn2, gf, batch, seq, sweep)
    ys, prs = _sample_path(x1s, prs, ob_head, wts, rel_bias, caches, page_table, g_ffn2, gf, n_seq, n_tok)
    return ((yp.reshape(batch, seq, d_model), ys.reshape(n_seq, n_tok, d_model))
            + _prompt_rows(prp, batch, seq) + _rows(prs, (n_seq, n_tok)))
```

```python
import functools
import math

import jax
import jax.numpy as jnp
import numpy as np
from jax import lax
from jax.experimental import pallas as pl
from jax.experimental.pallas import tpu as pltpu

F32 = jnp.float32
BF16 = jnp.bfloat16
I32 = jnp.int32

HEAD_DIM = 64
IDX_DIM = 64
N_HEADS_A = 8
N_KV_A = 2
GROUP_A = N_HEADS_A // N_KV_A
N_IDX_HEADS = 8
N_HEADS_B = 8
TOPK_MAX = 256
N_BUCKETS = 32
MAX_DISTANCE = 128
EPS = 1e-6
ATTN_SCALE = HEAD_DIM ** -0.5
IDX_SCALE = IDX_DIM ** -0.5
IDX_HEAD_SCALE = N_IDX_HEADS ** -0.5

LANES = 128
Q_TILE = 128
NEG = -1e30
INT_MIN = -2 ** 31
VMEM_LIMIT = 56 * 1024 * 1024
FFN_CHUNK = 256
TOKEN_TILE = 512
SUPER = 512
SUB = SUPER // LANES
TL_ROWS = LANES + 16


def _pick_tile(n, pref):
    t = min(n, pref)
    while n % t or t % 8:
        t -= 1
    return t


def _const_spec(shape):
    nd = len(shape)
    return pl.BlockSpec(shape, lambda *_: (0,) * nd, pipeline_mode=pl.Buffered(1))


def _dot(a, b):
    return jnp.dot(a, b, preferred_element_type=F32)


def _dot_nt(a, b):
    return lax.dot_general(a, b, (((1,), (1,)), ((), ())), preferred_element_type=F32)


def _rms(x, g):
    r = lax.rsqrt(jnp.mean(x * x, axis=-1, keepdims=True) + EPS)
    return (x * r) * g


def _softplus(z):
    return jnp.maximum(z, 0.0) + jnp.log(1.0 + jnp.exp(-jnp.abs(z)))


def _split_bf16(x):
    hi = x.astype(BF16)
    lo = (x - hi.astype(F32)).astype(BF16)
    return hi, lo


KEY_NEG_INF = INT_MIN + 0x7FFFFF


def _key_to_float(key):
    key = jnp.maximum(key, KEY_NEG_INF)
    return lax.bitcast_convert_type(key ^ ((key >> 31) & jnp.int32(0x7FFFFFFF)), F32)


def _search_threshold(count_ge, topk, shape):
    def bit_step(it, u):
        bit = lax.shift_left(jnp.int32(1), 31 - it)
        cand = _key_to_float((u | bit) ^ INT_MIN)
        return jnp.where(count_ge(cand) >= topk, u | bit, u)
    u = lax.fori_loop(0, 32, bit_step, jnp.zeros(shape, I32))
    return _key_to_float(u ^ INT_MIN)


def _swiglu(x, g_ref, wup_ref, wdn_ref, act_ref):
    d_ff = wdn_ref.shape[0]
    h = _rms(x, g_ref[...]).astype(BF16)
    for c in range(d_ff // FFN_CHUNK):
        lo = c * FFN_CHUNK
        gate = _dot(h, wup_ref[:, lo:lo + FFN_CHUNK])
        up = _dot(h, wup_ref[:, d_ff + lo:d_ff + lo + FFN_CHUNK])
        act_ref[:, lo:lo + FFN_CHUNK] = (gate * jax.nn.sigmoid(gate) * up).astype(BF16)
    return _dot(act_ref[...], wdn_ref[...])


def _ffn_kernel(x_ref, g_ref, wup_ref, wdn_ref, o_ref, act_ref):
    x = x_ref[...]
    o_ref[...] = x + 0.5 * _swiglu(x, g_ref, wup_ref, wdn_ref, act_ref)


def _ffn_call(x, g, wup, wdn):
    n, d = x.shape
    d_ff = wdn.shape[0]
    tm = _pick_tile(n, TOKEN_TILE)
    row = lambda w: pl.BlockSpec((tm, w), lambda i: (i, 0))
    return pl.pallas_call(
        _ffn_kernel,
        grid=(n // tm,),
        in_specs=[row(d), _const_spec(g.shape), _const_spec(wup.shape), _const_spec(wdn.shape)],
        out_specs=row(d),
        out_shape=jax.ShapeDtypeStruct((n, d), F32),
        scratch_shapes=[pltpu.VMEM((tm, d_ff), BF16)],
        compiler_params=pltpu.CompilerParams(dimension_semantics=("arbitrary",), vmem_limit_bytes=VMEM_LIMIT),
        name="ffn_pre",
    )(x, g, wup, wdn)


def _proj_kernel(x_ref, g_ref, wqa, wka, wva, wqi, wkw, wkk, wqb, wkb, wvb, wga, wgb,
                 qa, ka, va, kab, vab, qi, kw, kk, qb, kb, vb, kbb, vbb, sa, sb):
    h = _rms(x_ref[...], g_ref[...]).astype(BF16)
    qa[...] = _dot(h, wqa[...]).astype(BF16)
    k = _dot(h, wka[...])
    ka[...] = k
    kab[...] = k.astype(BF16)
    v = _dot(h, wva[...])
    va[...] = v
    vab[...] = v.astype(BF16)
    qi[...] = _dot(h, wqi[...]).astype(BF16)
    kw[...] = _dot(h, wkw[...])
    kk[...] = _dot(h, wkk[...]).astype(BF16)
    qb[...] = _dot(h, wqb[...]).astype(BF16)
    k = _dot(h, wkb[...])
    kb[...] = k
    kbb[...] = k.astype(BF16)
    v = _dot(h, wvb[...])
    vb[...] = v
    vbb[...] = v.astype(BF16)
    sa[...] = jax.nn.sigmoid(_dot(h, wga[...]))
    sb[...] = jax.nn.sigmoid(_dot(h, wgb[...]))


_PROJ_OUT = (("qa", 512, BF16), ("ka", 128, F32), ("va", 128, F32), ("kab", 128, BF16), ("vab", 128, BF16),
             ("qi", 512, BF16), ("kw", 128, F32), ("kk", 128, BF16), ("qb", 512, BF16),
             ("kb", 512, F32), ("vb", 512, F32), ("kbb", 512, BF16), ("vbb", 512, BF16),
             ("sa", 1024, F32), ("sb", 1024, F32))


def _proj_call(x, g, weights):
    n, d = x.shape
    tm = _pick_tile(n, TOKEN_TILE)
    row = lambda w: pl.BlockSpec((tm, w), lambda i: (i, 0))
    outs = pl.pallas_call(
        _proj_kernel,
        grid=(n // tm,),
        in_specs=[row(d), _const_spec(g.shape)] + [_const_spec(w.shape) for w in weights],
        out_specs=[row(w) for _, w, _ in _PROJ_OUT],
        out_shape=[jax.ShapeDtypeStruct((n, w), dt) for _, w, dt in _PROJ_OUT],
        compiler_params=pltpu.CompilerParams(dimension_semantics=("arbitrary",), vmem_limit_bytes=VMEM_LIMIT),
        name="proj",
    )(x, g, *weights)
    return dict(zip([nm for nm, _, _ in _PROJ_OUT], outs))


_PROMPT_ROW_OUT = (("qa", 512, BF16), ("kab", 128, BF16), ("qi", 512, BF16), ("kk", 128, BF16),
                   ("qb", 512, BF16), ("kbb", 512, BF16), ("sa", 1024, F32), ("sb", 1024, F32))
_PROMPT_COL_OUT = (("kat", 0, 128), ("vat", 128, 128), ("kwt", 256, 128), ("kbt", 384, 512), ("vbt", 896, 512))


def _proj_prompt_work(x_ref, g_ref, weights, wt_ref, outs):
    wqa, wka, wqi, wkk, wqb, wkb, wga, wgb = weights
    qa, kab, qi, kk, qb, kbb, sa, sb, kat, vat, kwt, kbt, vbt, vatb, vbtb = outs
    h = _rms(x_ref[...], g_ref[...]).astype(BF16)

    def rows_bf16(out, w):
        def item():
            out[...] = _dot(h, w[...]).astype(BF16)
        return item

    def gate(out, w):
        def item():
            out[...] = jax.nn.sigmoid(_dot(h, w[...]))
        return item

    def cols():
        t = _dot_nt(wt_ref[...], h)
        for ref, (_, r0, nr) in zip((kat, vat, kwt, kbt, vbt), _PROMPT_COL_OUT):
            ref[0] = t[r0:r0 + nr]
        vatb[0, 0] = t[128:256].astype(BF16)
        vbtb[0, 0] = t[896:1408].astype(BF16)
    return [rows_bf16(qa, wqa), rows_bf16(kab, wka), rows_bf16(qi, wqi), gate(sa, wga), rows_bf16(kk, wkk),
            rows_bf16(qb, wqb), gate(sb, wgb), rows_bf16(kbb, wkb), cols]


def _proj_prompt_kernel(x_ref, g_ref, *rest):
    for item in _proj_prompt_work(x_ref, g_ref, rest[:8], rest[8], rest[9:]):
        item()


def _proj_prompt_call(x, g, weights, wt_all, batch, seq):
    n, d = x.shape
    tm = SUPER
    per_b = seq // tm
    row = lambda w: pl.BlockSpec((tm, w), lambda i: (i, 0))
    col = lambda r: pl.BlockSpec((1, r, tm), lambda i: (i // per_b, 0, i % per_b))
    blk = lambda r: pl.BlockSpec((1, 1, r, tm), lambda i: (i // per_b, i % per_b, 0, 0))
    in_specs = [row(d), _const_spec(g.shape)] + [_const_spec(w.shape) for w in weights] + [_const_spec(wt_all.shape)]
    out_specs = ([row(w) for _, w, _ in _PROMPT_ROW_OUT] + [col(nr) for _, _, nr in _PROMPT_COL_OUT]
                 + [blk(128), blk(512)])
    out_shape = ([jax.ShapeDtypeStruct((n, w), dt) for _, w, dt in _PROMPT_ROW_OUT]
                 + [jax.ShapeDtypeStruct((batch, nr, seq), F32) for _, _, nr in _PROMPT_COL_OUT]
                 + [jax.ShapeDtypeStruct((batch, per_b, 128, tm), BF16),
                    jax.ShapeDtypeStruct((batch, per_b, 512, tm), BF16)])
    names = [nm for nm, _, _ in _PROMPT_ROW_OUT] + [nm for nm, _, _ in _PROMPT_COL_OUT] + ["vatb", "vbtb"]
    outs = pl.pallas_call(
        _proj_prompt_kernel, grid=(n // tm,), in_specs=in_specs, out_specs=out_specs, out_shape=out_shape,
        compiler_params=pltpu.CompilerParams(dimension_semantics=("arbitrary",), vmem_limit_bytes=VMEM_LIMIT),
        name="proj_prompt",
    )(x, g, *weights, wt_all)
    return dict(zip(names, outs))


def _post_kernel(x_ref, oa_ref, ob_ref, sa_ref, sb_ref, woa, wob, wout, g2_ref, wup_ref, wdn_ref, gf_ref,
                 y_ref, act_ref):
    mix = sa_ref[...] * _dot(oa_ref[...], woa[...]) + sb_ref[...] * _dot(ob_ref[...], wob[...])
    x2 = x_ref[...] + _dot(mix.astype(BF16), wout[...])
    x3 = x2 + 0.5 * _swiglu(x2, g2_ref, wup_ref, wdn_ref, act_ref)
    y_ref[...] = _rms(x3, gf_ref[...])


def _post_call(x, oa, ob, sa, sb, woa, wob, wout, g2, wup, wdn, gf):
    n, d = x.shape
    d_ff = wdn.shape[0]
    tm = _pick_tile(n, TOKEN_TILE)
    row = lambda w: pl.BlockSpec((tm, w), lambda i: (i, 0))
    consts = (woa, wob, wout, g2, wup, wdn, gf)
    return pl.pallas_call(
        _post_kernel,
        grid=(n // tm,),
        in_specs=[row(d), row(oa.shape[1]), row(ob.shape[1]), row(d), row(d)] + [_const_spec(c.shape) for c in consts],
        out_specs=row(d),
        out_shape=jax.ShapeDtypeStruct((n, d), F32),
        scratch_shapes=[pltpu.VMEM((tm, d_ff), BF16)],
        compiler_params=pltpu.CompilerParams(dimension_semantics=("arbitrary",), vmem_limit_bytes=VMEM_LIMIT),
        name="post",
    )(x, oa, ob, sa, sb, *consts)


def _t5_bucket_np(dist):
    max_exact = N_BUCKETS // 2
    d = np.maximum(dist, 0)
    ratio = np.maximum(d, 1).astype(np.float32) / np.float32(max_exact)
    large = max_exact + (np.log(ratio) / np.float32(math.log(MAX_DISTANCE / max_exact))
                         * np.float32(N_BUCKETS - max_exact)).astype(np.int32)
    large = np.minimum(large, N_BUCKETS - 1)
    return np.where(d < max_exact, d, large).astype(np.int32)


def _bias_kernel(rel_ref, idx_ref, out_ref):
    idx = idx_ref[...]
    for h in range(out_ref.shape[0]):
        acc = jnp.zeros(idx.shape, F32)
        for b in range(N_BUCKETS):
            acc = jnp.where(idx == b, rel_ref[b, h], acc)
        out_ref[h] = acc


def _bias_call(rel_bias, idx):
    return pl.pallas_call(
        _bias_kernel,
        in_specs=[pl.BlockSpec(memory_space=pltpu.SMEM), pl.BlockSpec(memory_space=pltpu.VMEM)],
        out_specs=pl.BlockSpec(memory_space=pltpu.VMEM),
        out_shape=jax.ShapeDtypeStruct((rel_bias.shape[1],) + idx.shape, F32),
        name="rel_bias_table",
    )(rel_bias, jnp.asarray(idx))


def _half_masks():
    lane = lax.broadcasted_iota(I32, (Q_TILE, LANES), 1)
    return lane < HEAD_DIM


def _store_masked_pairs(src_ref, dst_ref, n_pairs, scale=1.0):
    lo_half = _half_masks()
    for p in range(n_pairs):
        pair = src_ref[:, p * LANES:(p + 1) * LANES].astype(F32) * scale
        dst_ref[p, 0:Q_TILE, :] = jnp.where(lo_half, pair, 0.0).astype(BF16)
        dst_ref[p, Q_TILE:2 * Q_TILE, :] = jnp.where(lo_half, 0.0, pair).astype(BF16)


def _tri_ones(strict_upper):
    r = np.arange(LANES)
    if strict_upper:
        tri = (r[:, None] < r[None, :])
    else:
        tri = (r[:, None] > r[None, :])
    return jnp.asarray(np.concatenate([tri, np.ones((LANES, LANES), bool)], axis=1), dtype=BF16)


N_DSA_PHASES = 4


def _dsa_prompt_sb_kernel(pt_ref, qi_ref, kwt_ref, kk_ref, qa_ref, ka_ref, vat_ref, bias_ref, lt_ref, *rest,
                          topk, n_pages, page, chunk):
    sweep_in, (o_ref, ob_ref), scratch = rest[:6], rest[6:8], rest[8:]
    s = pl.program_id(0) * pl.num_programs(1) + pl.program_id(1)
    last = pl.num_programs(0) * pl.num_programs(1) - 1
    prologue, sweep_chunk, epilogue, n_chunks = _sweep_hooks(s, last, pt_ref, sweep_in, ob_ref, scratch[8:],
                                                              n_pages, page, chunk)
    at = {(c * N_DSA_PHASES) // n_chunks: c for c in range(n_chunks)}
    prologue()
    _dsa_prompt_kernel(qi_ref, kwt_ref, kk_ref, qa_ref, ka_ref, vat_ref, bias_ref, lt_ref, o_ref, *scratch[:8],
                       topk=topk, phase_hook=lambda ph: sweep_chunk(at[ph]) if ph in at else None)
    epilogue()


def _dsa_prompt_kernel(qi_ref, kwt_ref, kk_ref, qa_ref, ka_ref, vat_ref, bias_ref, lt_ref, o_ref,
                       qim_ref, qam_ref, key_ref, sel_ref, lg_ref, mx_ref, ls_ref, out_ref, *, topk,
                       phase_hook=lambda ph: None):
    i = pl.program_id(1)
    ntile = i + 1
    n_super = i // SUB + 1
    n_pairs = N_HEADS_A // 2
    row = lax.broadcasted_iota(I32, (LANES, LANES), 0)
    lane = lax.broadcasted_iota(I32, (LANES, LANES), 1)
    key_minus_query = row - lane

    phase_hook(0)
    _store_masked_pairs(qi_ref, qim_ref, N_IDX_HEADS // 2, IDX_SCALE)
    _store_masked_pairs(qa_ref, qam_ref, n_pairs, ATTN_SCALE)
    wt = kwt_ref[0, IDX_DIM:IDX_DIM + N_IDX_HEADS, :] * IDX_HEAD_SCALE
    mx_ref[...] = jnp.full(mx_ref.shape, NEG, F32)
    ls_ref[...] = jnp.zeros(ls_ref.shape, F32)
    out_ref[...] = jnp.zeros(out_ref.shape, F32)

    def super_rows(jt):
        return pl.ds(pl.multiple_of(jt * SUPER, SUPER), SUPER)

    def valid_tile(j):
        return key_minus_query <= (i - j) * LANES

    def score_super(jt, c):
        kt = kk_ref[super_rows(jt), :]
        acc = jnp.zeros((SUPER, LANES), F32)
        for p in range(N_IDX_HEADS // 2):
            s = _dot_nt(kt, qim_ref[p])
            acc = acc + wt[2 * p:2 * p + 1] * jnp.maximum(s[:, :LANES], 0.0)
            acc = acc + wt[2 * p + 1:2 * p + 2] * jnp.maximum(s[:, LANES:], 0.0)
        for k in range(SUB):
            j = jt * SUB + k
            key_ref[j] = jnp.where(valid_tile(j), acc[k * LANES:(k + 1) * LANES], -jnp.inf)
        return c
    lax.fori_loop(0, n_super, score_super, 0)
    phase_hook(1)

    def count_tiles(pred):
        def body(jt, cnt):
            for k in range(SUB):
                cnt = cnt + jnp.where(pred(key_ref[jt * SUB + k]), 1, 0)
            return cnt
        cnt = lax.fori_loop(0, n_super, body, jnp.zeros((LANES, LANES), I32))
        return jnp.sum(cnt, axis=0, keepdims=True)

    thr = _search_threshold(lambda cand: count_tiles(lambda k: k >= cand), topk, (1, LANES))
    need = (topk - count_tiles(lambda k: k > thr)).astype(F32)

    over = jnp.logical_and(count_tiles(lambda k: k >= thr) > topk, thr > -jnp.inf)
    ties = jnp.sum(jnp.where(over, 1, 0)) > 0

    @pl.when(ties)
    def _():
        def select_tile(j, carry):
            key = key_ref[j]
            eq = key == thr
            cs = _dot(lt_ref[...], jnp.where(eq, 1.0, 0.0).astype(BF16))
            take = jnp.logical_or(key > thr, jnp.logical_and(eq, cs[:LANES] + carry < need))
            sel_ref[j] = jnp.where(jnp.logical_and(take, valid_tile(j)), 0.0, NEG)
            return carry + cs[LANES:LANES + 1]
        lax.fori_loop(0, n_super * SUB, select_tile, jnp.zeros((1, LANES), F32))

    @pl.when(jnp.logical_not(ties))
    def _():
        def select_tile(j, c):
            take = jnp.logical_and(key_ref[j] >= thr, valid_tile(j))
            sel_ref[j] = jnp.where(take, 0.0, NEG)
            return c
        lax.fori_loop(0, n_super * SUB, select_tile, 0)
    phase_hook(2)

    def logit_super(jt, c):
        kt = ka_ref[super_rows(jt), :]
        for p in range(n_pairs):
            s = _dot_nt(kt, qam_ref[p])
            mx = mx_ref[p]
            for k in range(SUB):
                j = jt * SUB + k
                rel = jnp.clip(i - j, 0, 2)
                sel = sel_ref[j]
                sk = s[k * LANES:(k + 1) * LANES]
                lg0 = sk[:, :LANES] + bias_ref[p, rel] + sel
                lg1 = sk[:, LANES:] + bias_ref[GROUP_A + p, rel] + sel
                lg_ref[p, j, :, 0:LANES] = lg0
                lg_ref[p, j, :, LANES:2 * LANES] = lg1
                lg = jnp.concatenate([lg0, lg1], axis=1)
                mx = jnp.maximum(mx, jnp.max(lg.reshape(LANES // 8, 8, 2 * LANES), axis=0))
            mx_ref[p] = mx
        return c
    lax.fori_loop(0, n_super, logit_super, 0)

    for p in range(n_pairs):
        m = jnp.max(mx_ref[p], axis=0, keepdims=True)
        mx_ref[p] = jnp.broadcast_to(m, mx_ref.shape[1:])
    phase_hook(3)

    def value_super(jt, c):
        vt = vat_ref[0, jt]
        for p in range(n_pairs):
            m = mx_ref[p][0:1]
            parts = []
            ls = ls_ref[p]
            for k in range(SUB):
                pk = jnp.exp(lg_ref[p, jt * SUB + k] - m)
                ls = ls + jnp.sum(pk.reshape(LANES // 8, 8, 2 * LANES), axis=0)
                parts.append(pk.astype(BF16))
            ls_ref[p] = ls
            out_ref[p] += _dot(vt, jnp.concatenate(parts, axis=0))
        return c
    lax.fori_loop(0, n_super, value_super, 0)

    for p in range(n_pairs):
        denom = jnp.sum(ls_ref[p], axis=0, keepdims=True)
        o_ref[:, p * LANES:(p + 1) * LANES] = _pair_transposed_out(out_ref[p] / denom).astype(BF16)


def _tri_prefix_ones():
    r = np.arange(LANES)
    tri = (r[None, :] < r[:, None])
    return jnp.asarray(np.concatenate([tri, np.ones((TL_ROWS - LANES, LANES), bool)], axis=0), dtype=BF16)


def _dsa_prompt_call(pr, kwt, vat, bias3, batch, seq, sweep=None):
    nq = seq // Q_TILE
    n_pairs = N_HEADS_A // 2
    topk = min(TOPK_MAX, seq // 4)
    lt = _tri_prefix_ones()
    ix = (lambda f: (lambda b, i, pt: f(b, i))) if sweep is not None else (lambda f: f)
    qblk = lambda w: pl.BlockSpec((Q_TILE, w), ix(lambda b, i: (b * nq + i, 0)))
    kblk = lambda w: pl.BlockSpec((seq, w), ix(lambda b, i: (b, 0)))
    const = lambda a: pl.BlockSpec(a.shape, ix(lambda b, i: (0,) * a.ndim), pipeline_mode=pl.Buffered(1))
    in_specs = [qblk(512), pl.BlockSpec((1, LANES, Q_TILE), ix(lambda b, i: (b, 0, i))), kblk(LANES),
                qblk(512), kblk(LANES), pl.BlockSpec((1,) + vat.shape[1:], ix(lambda b, i: (b, 0, 0, 0))),
                const(bias3), const(lt)]
    scratch = [
        pltpu.VMEM((N_IDX_HEADS // 2, 2 * Q_TILE, LANES), BF16),
        pltpu.VMEM((n_pairs, 2 * Q_TILE, LANES), BF16),
        pltpu.VMEM((nq, LANES, Q_TILE), F32),
        pltpu.VMEM((nq, LANES, Q_TILE), F32),
        pltpu.VMEM((n_pairs, nq, LANES, 2 * Q_TILE), F32),
        pltpu.VMEM((n_pairs, 8, 2 * Q_TILE), F32),
        pltpu.VMEM((n_pairs, 8, 2 * Q_TILE), F32),
        pltpu.VMEM((n_pairs, LANES, 2 * Q_TILE), F32),
    ]
    out_shape = jax.ShapeDtypeStruct((batch * seq, 512), BF16)
    params = pltpu.CompilerParams(dimension_semantics=("arbitrary", "arbitrary"), vmem_limit_bytes=VMEM_LIMIT)
    args = (pr["qi"], kwt, pr["kk"], pr["qa"], pr["kab"], vat, bias3, lt)
    if sweep is None:
        oa = pl.pallas_call(
            functools.partial(_dsa_prompt_kernel, topk=topk), grid=(batch, nq), in_specs=in_specs,
            out_specs=qblk(512), out_shape=out_shape, scratch_shapes=scratch, compiler_params=params,
            name="dsa_prompt",
        )(*args)
        return oa, None
    page_table, qb4, kb_new, vb_new, cache_k_b, cache_v_b, to = sweep
    n_seq, n_pages = page_table.shape
    assert n_seq == batch * nq
    chunk = _sweep_chunk_size(page_table, cache_k_b)
    sweep_in, tok, sweep_scratch = _sweep_specs(chunk, to, lambda b, i: b * nq + i)
    return pl.pallas_call(
        functools.partial(_dsa_prompt_sb_kernel, topk=topk, n_pages=n_pages, page=cache_k_b.shape[-1], chunk=chunk),
        grid_spec=pltpu.PrefetchScalarGridSpec(
            num_scalar_prefetch=1, grid=(batch, nq), in_specs=in_specs + sweep_in,
            out_specs=[qblk(512), tok], scratch_shapes=scratch + sweep_scratch),
        out_shape=[out_shape, jax.ShapeDtypeStruct((n_seq, 4, N_HEADS_B * HEAD_DIM), F32)],
        compiler_params=params, name="dsa_prompt_sb_sweep",
    )(page_table, *args, qb4, kb_new, vb_new, cache_k_b, cache_v_b, to)


def _tri_lower_ones():
    r = np.arange(LANES)
    tri = (r[None, :] > r[:, None])
    top = np.concatenate([tri, tri], axis=1)
    full = np.concatenate([top, np.ones((TL_ROWS - LANES, 2 * LANES), bool)], axis=0)
    return jnp.asarray(-full.astype(np.float32), dtype=BF16)


def _pair_transposed_out(acc):
    top = lax.broadcasted_iota(I32, (LANES, LANES), 0) < HEAD_DIM
    return jnp.where(top, acc[:, :LANES], acc[:, LANES:]).T


def _sb_prompt_kernel(qb_ref, kb_ref, vbt_ref, tl_ref, o_ref, qm_ref, z_ref, hl_ref, cs_ref, a_ref, run_ref, out_ref):
    i = pl.program_id(1)
    n_pairs = N_HEADS_B // 2
    last = i // SUB

    _store_masked_pairs(qb_ref, qm_ref, n_pairs, ATTN_SCALE)
    run_ref[...] = jnp.zeros(run_ref.shape, F32)
    out_ref[...] = jnp.zeros(out_ref.shape, F32)

    key_row = lax.broadcasted_iota(I32, (SUPER, 2 * LANES), 0)
    q_lane = lax.broadcasted_iota(I32, (SUPER, 2 * LANES), 1) % LANES
    strict = key_row < (i % SUB) * LANES + q_lane

    def super_tile(jt, diag):
        rows = pl.ds(pl.multiple_of(jt * SUPER, SUPER), SUPER)
        for p in range(n_pairs):
            z_ref[p] = _dot_nt(kb_ref[rows, p * LANES:(p + 1) * LANES], qm_ref[p])
        for p in range(n_pairs):
            z = z_ref[p]
            sp = _softplus(z)
            hi, lo = _split_bf16(jnp.where(strict, sp, 0.0) if diag else sp)
            for k in range(SUB):
                ks = slice(k * LANES, (k + 1) * LANES)
                hl_ref[p, :, k * 2 * LANES:(k + 1) * 2 * LANES] = jnp.concatenate([hi[ks], lo[ks]], axis=0)
            z_ref[p] = z - sp
        for p in range(n_pairs):
            cs_ref[p] = _dot(tl_ref[...], hl_ref[p])
        for p in range(n_pairs):
            run = run_ref[p]
            for k in reversed(range(SUB)):
                cs = cs_ref[p, :, k * 2 * LANES:(k + 1) * 2 * LANES]
                ks = slice(k * LANES, (k + 1) * LANES)
                a = jnp.exp(z_ref[p, ks, :] + cs[:LANES] + run)
                if diag:
                    a = jnp.where(strict[ks], a, 0.0)
                a_ref[p, ks, :] = a.astype(BF16)
                run = run + cs[LANES:LANES + 1]
            run_ref[p] = run
        for p in range(n_pairs):
            out_ref[p] += _dot(vbt_ref[0, jt, p * LANES:(p + 1) * LANES, :], a_ref[p])

    super_tile(last, True)

    def older(jj, c):
        super_tile(last - 1 - jj, False)
        return c
    lax.fori_loop(0, last, older, 0)

    for p in range(n_pairs):
        o_ref[:, p * LANES:(p + 1) * LANES] = _pair_transposed_out(out_ref[p]).astype(BF16)


def _sb_prompt_call(pr, vbt, tl, batch, seq):
    nq = seq // Q_TILE
    width = N_HEADS_B * HEAD_DIM
    n_pairs = N_HEADS_B // 2
    qblk = pl.BlockSpec((Q_TILE, width), lambda b, i: (b * nq + i, 0))
    kblk = pl.BlockSpec((seq, width), lambda b, i: (b, 0))
    vblk = pl.BlockSpec((1,) + vbt.shape[1:], lambda b, i: (b, 0, 0, 0))
    return pl.pallas_call(
        _sb_prompt_kernel,
        grid=(batch, nq),
        in_specs=[qblk, kblk, vblk, _const_spec(tl.shape)],
        out_specs=qblk,
        out_shape=jax.ShapeDtypeStruct((batch * seq, width), BF16),
        scratch_shapes=[
            pltpu.VMEM((n_pairs, 2 * Q_TILE, LANES), BF16),
            pltpu.VMEM((n_pairs, SUPER, 2 * LANES), F32),
            pltpu.VMEM((n_pairs, 2 * LANES, SUB * 2 * LANES), BF16),
            pltpu.VMEM((n_pairs, TL_ROWS, SUB * 2 * LANES), F32),
            pltpu.VMEM((n_pairs, SUPER, 2 * LANES), BF16),
            pltpu.VMEM((n_pairs, 1, 2 * LANES), F32),
            pltpu.VMEM((n_pairs, LANES, 2 * LANES), F32),
        ],
        compiler_params=pltpu.CompilerParams(dimension_semantics=("arbitrary", "arbitrary"),
                                             vmem_limit_bytes=VMEM_LIMIT),
        name="sb_prompt",
    )(pr["qb"], pr["kbb"], vbt, tl)


SEQ_GROUP = 4
KEY_CHUNK = 1024
SB_CHUNK = 2048


def _page_copy(cache_ref, buf_ref, sem, phys, p, page):
    return pltpu.make_async_copy(cache_ref.at[0, phys], buf_ref.at[:, p * page:(p + 1) * page], sem)


def _start_pages(cache_ref, buf_ref, sem, pt_ref, seq, first_page, n_copy, page):
    for p in range(n_copy):
        _page_copy(cache_ref, buf_ref, sem, pt_ref[seq, first_page + p], p, page).start()


def _wait_pages(cache_ref, buf_ref, sem, n_copy, page):
    for p in range(n_copy):
        _page_copy(cache_ref, buf_ref, sem, 0, p, page).wait()


def _dsa_sample_select_kernel(pt_ref, qi_ref, w_ref, kn_ref, cache_ref, uo_ref, sel_ref,
                              kbuf, sem, key_ref, knew_ref, *, topk, n_pages, page):
    s = pl.program_id(0)
    nstep = pl.num_programs(0)
    slot = s % 2
    past = n_pages * page
    n_tiles = past // LANES + 1
    rows = SEQ_GROUP * 4

    def start(step, sl):
        for g in range(SEQ_GROUP):
            _start_pages(cache_ref, kbuf.at[sl, g], sem.at[sl], pt_ref, step * SEQ_GROUP + g, 0, n_pages, page)

    @pl.when(s == 0)
    def _():
        start(s, slot)

    @pl.when(s + 1 < nstep)
    def _():
        start(s + 1, 1 - slot)

    for g in range(SEQ_GROUP):
        _wait_pages(cache_ref, kbuf.at[slot, g], sem.at[slot], n_pages, page)

    q = qi_ref[...]
    w = w_ref[...] * IDX_HEAD_SCALE
    row = lax.broadcasted_iota(I32, (rows, LANES), 0)
    lane = lax.broadcasted_iota(I32, (rows, LANES), 1)

    def head_sum(sc):
        sc = jnp.maximum(sc * IDX_SCALE, 0.0) * w
        return jnp.sum(sc.reshape(N_IDX_HEADS, rows, sc.shape[-1]), axis=0)

    def own_rows(parts):
        r = lax.broadcasted_iota(I32, parts[0].shape, 0)
        sc = parts[-1]
        for g in reversed(range(SEQ_GROUP - 1)):
            sc = jnp.where(r < 4 * (g + 1), parts[g], sc)
        return sc

    for c in range(past // KEY_CHUNK):
        parts = []
        for g in range(SEQ_GROUP):
            kc = kbuf[slot, g, :, c * KEY_CHUNK:(c + 1) * KEY_CHUNK].astype(BF16)
            parts.append(head_sum(_dot(q, kc)))
        key_ref[:, c * KEY_CHUNK:(c + 1) * KEY_CHUNK] = own_rows(parts)

    parts = []
    for g in range(SEQ_GROUP):
        knew_ref[...] = jnp.zeros(knew_ref.shape, F32)
        knew_ref[0:4, :] = kn_ref[g]
        parts.append(head_sum(_dot_nt(q, knew_ref[...].astype(BF16))))
    sc = own_rows(parts)
    valid_new = jnp.logical_and(lane <= row % 4, lane < 4)
    key_ref[:, past:past + LANES] = jnp.where(valid_new, sc, -jnp.inf)

    key = key_ref[...]
    thr = _search_threshold(lambda cand: jnp.sum(jnp.where(key >= cand, 1, 0), axis=1, keepdims=True), topk, (rows, 1))
    need =(topk - jnp.sum(jnp.where(key > thr, 1, 0), axis=1, keepdims=True)).astype(F32)

    eq = jnp.where(key == thr, 1.0, 0.0)
    stack = jnp.concatenate([eq[:, t * LANES:(t + 1) * LANES] for t in range(n_tiles)], axis=0).astype(BF16)
    cs = _dot(stack, uo_ref[...])
    carry = jnp.zeros((rows, LANES), F32)
    for t in range(n_tiles):
        kt = key[:, t * LANES:(t + 1) * LANES]
        pre = cs[t * rows:(t + 1) * rows, :LANES] + carry
        carry = carry + cs[t * rows:(t + 1) * rows, LANES:]
        take = jnp.logical_or(kt > thr, jnp.logical_and(kt == thr, pre < need))
        if t == n_tiles - 1:
            take = jnp.logical_and(take, valid_new)
        sel_ref[:, t * LANES:(t + 1) * LANES] = jnp.where(take, 0.0, NEG)


def _dsa_sample_select_call(page_table, qi64, wcol, ki_new, cache_k_idx, uo, topk):
    n, n_pages = page_table.shape
    page = cache_k_idx.shape[-1]
    past = n_pages * page
    width = past + LANES
    rows = SEQ_GROUP * 4
    return pl.pallas_call(
        functools.partial(_dsa_sample_select_kernel, topk=topk, n_pages=n_pages, page=page),
        grid_spec=pltpu.PrefetchScalarGridSpec(
            num_scalar_prefetch=1,
            grid=(n // SEQ_GROUP,),
            in_specs=[pl.BlockSpec((N_IDX_HEADS * rows, IDX_DIM), lambda s, pt: (s, 0)),
                      pl.BlockSpec((N_IDX_HEADS * rows, 1), lambda s, pt: (s, 0)),
                      pl.BlockSpec((SEQ_GROUP, 4, IDX_DIM), lambda s, pt: (s, 0, 0)),
                      pl.BlockSpec(memory_space=pl.ANY),
                      pl.BlockSpec(uo.shape, lambda s, pt: (0, 0))],
            out_specs=pl.BlockSpec((rows, width), lambda s, pt: (s, 0)),
            scratch_shapes=[pltpu.VMEM((2, SEQ_GROUP, IDX_DIM, past), F32),
                            pltpu.SemaphoreType.DMA((2,)),
                            pltpu.VMEM((rows, width), F32),
                            pltpu.VMEM((LANES, IDX_DIM), F32)]),
        out_shape=jax.ShapeDtypeStruct((n * 4, width), F32),
        compiler_params=pltpu.CompilerParams(dimension_semantics=("arbitrary",), vmem_limit_bytes=VMEM_LIMIT),
        name="dsa_sample_select",
    )(page_table, qi64, wcol, ki_new, cache_k_idx, uo)


def _dsa_sample_attn_kernel(pt_ref, qa_ref, sel_ref, bias_ref, cb_ref, kn_ref, vn_ref, kcache, vcache, o_ref,
                            kbuf, vbuf, sem, lg_ref, new_ref, *, n_pages, page):
    n = pl.program_id(0)
    nseq = pl.num_programs(0)
    slot = n % 2
    past = n_pages * page
    n_chunks = past // KEY_CHUNK
    rows = N_HEADS_A * 4

    def start(seq, sl):
        _start_pages(kcache, kbuf.at[sl], sem.at[0, sl], pt_ref, seq, 0, n_pages, page)
        _start_pages(vcache, vbuf.at[sl], sem.at[1, sl], pt_ref, seq, 0, n_pages, page)

    @pl.when(n == 0)
    def _():
        start(n, slot)

    @pl.when(n + 1 < nseq)
    def _():
        start(n + 1, 1 - slot)

    lo_half = lax.broadcasted_iota(I32, (rows // 2, LANES), 1) < HEAD_DIM
    q = qa_ref[...].astype(F32)
    q32 = jnp.concatenate([jnp.where(lo_half, q, 0.0), jnp.where(lo_half, 0.0, q)], axis=0).astype(BF16)

    r = lax.broadcasted_iota(I32, (rows, SEQ_GROUP * 4), 0)
    c = lax.broadcasted_iota(I32, (rows, SEQ_GROUP * 4), 1)
    pick = jnp.where(c == (n % SEQ_GROUP) * 4 + (r % 16) // 4, 1.0, 0.0).astype(BF16)
    sel = _dot(pick, sel_ref[...].astype(BF16))

    cb = cb_ref[...]
    bias = bias_ref[...]
    _wait_pages(kcache, kbuf.at[slot], sem.at[0, slot], n_pages, page)
    mx = jnp.full((rows, LANES), NEG, F32)
    for ch in range(n_chunks):
        kc = kbuf[slot, :, ch * KEY_CHUNK:(ch + 1) * KEY_CHUNK].astype(BF16)
        lg = _dot(q32, kc) * ATTN_SCALE + cb + sel[:, ch * KEY_CHUNK:(ch + 1) * KEY_CHUNK]
        if ch == n_chunks - 1:
            near = jnp.concatenate([jnp.zeros((rows, KEY_CHUNK - LANES), F32), bias[:, :LANES] - cb], axis=1)
            lg = lg + near
        lg_ref[ch] = lg
        for t in range(KEY_CHUNK // LANES):
            mx = jnp.maximum(mx, lg[:, t * LANES:(t + 1) * LANES])
    new_ref[...] = jnp.zeros(new_ref.shape, F32)
    new_ref[0:4, :] = kn_ref[0]
    lg_new = _dot_nt(q32, new_ref[...].astype(BF16)) * ATTN_SCALE + bias[:, LANES:] + sel[:, past:]
    m = jnp.max(jnp.maximum(mx, lg_new), axis=1, keepdims=True)

    _wait_pages(vcache, vbuf.at[slot], sem.at[1, slot], n_pages, page)
    new_ref[0:4, :] = vn_ref[0]
    p_new = jnp.exp(lg_new - m)
    out = _dot(p_new.astype(BF16), new_ref[...].astype(BF16))
    lsum = p_new
    for ch in range(n_chunks):
        pr = jnp.exp(lg_ref[ch] - m)
        out = out + _dot_nt(pr.astype(BF16), vbuf[slot, :, ch * KEY_CHUNK:(ch + 1) * KEY_CHUNK].astype(BF16))
        for t in range(KEY_CHUNK // LANES):
            lsum = lsum + pr[:, t * LANES:(t + 1) * LANES]
    out = out / jnp.sum(lsum, axis=1, keepdims=True)
    o_ref[...] = jnp.where(lo_half, out[:rows // 2], out[rows // 2:]).astype(BF16)


def _dsa_sample_attn_call(page_table, qa16, sel, bias32, cb32, ka_new, va_new, cache_k_a, cache_v_a):
    n, n_pages = page_table.shape
    page = cache_k_a.shape[-1]
    past = n_pages * page
    width = past + LANES
    rows = N_HEADS_A * 4
    kv_w = N_KV_A * HEAD_DIM
    return pl.pallas_call(
        functools.partial(_dsa_sample_attn_kernel, n_pages=n_pages, page=page),
        grid_spec=pltpu.PrefetchScalarGridSpec(
            num_scalar_prefetch=1,
            grid=(n,),
            in_specs=[pl.BlockSpec((rows // 2, LANES), lambda s, pt: (s, 0)),
                      pl.BlockSpec((SEQ_GROUP * 4, width), lambda s, pt: (s // SEQ_GROUP, 0)),
                      pl.BlockSpec(bias32.shape, lambda s, pt: (0, 0)),
                      pl.BlockSpec(cb32.shape, lambda s, pt: (0, 0)),
                      pl.BlockSpec((1, 4, kv_w), lambda s, pt: (s, 0, 0)),
                      pl.BlockSpec((1, 4, kv_w), lambda s, pt: (s, 0, 0)),
                      pl.BlockSpec(memory_space=pl.ANY),
                      pl.BlockSpec(memory_space=pl.ANY)],
            out_specs=pl.BlockSpec((rows // 2, LANES), lambda s, pt: (s, 0)),
            scratch_shapes=[pltpu.VMEM((2, kv_w, past), F32),
                            pltpu.VMEM((2, kv_w, past), F32),
                            pltpu.SemaphoreType.DMA((2, 2)),
                            pltpu.VMEM((past // KEY_CHUNK, rows, KEY_CHUNK), F32),
                            pltpu.VMEM((LANES, kv_w), F32)]),
        out_shape=jax.ShapeDtypeStruct((n * rows // 2, LANES), BF16),
        compiler_params=pltpu.CompilerParams(dimension_semantics=("arbitrary",), vmem_limit_bytes=VMEM_LIMIT),
        name="dsa_sample_attn",
    )(page_table, qa16, sel, bias32, cb32, ka_new, va_new, cache_k_a, cache_v_a)


def _sb_new_tokens(qb_ref, kn_ref, vn_ref, to_ref, qbd_ref, acc_ref, oacc_ref, new_ref):
    rows, width = qbd_ref.shape
    q = qb_ref[0].astype(F32)
    q32 = jnp.concatenate([jnp.broadcast_to(q[t:t + 1, :], (N_HEADS_B, width)) for t in range(4)], axis=0)
    r = lax.broadcasted_iota(I32, (rows, width), 0)
    l = lax.broadcasted_iota(I32, (rows, width), 1)
    qbd_ref[...] = jnp.where(l // HEAD_DIM == r % N_HEADS_B, q32, 0.0).astype(BF16)
    row = lax.broadcasted_iota(I32, (rows, LANES), 0)
    lane = lax.broadcasted_iota(I32, (rows, LANES), 1)
    mask = jnp.logical_and(lane < row // N_HEADS_B, lane < 4)
    new_ref[...] = jnp.zeros(new_ref.shape, F32)
    new_ref[0:4, :] = kn_ref[0]
    z = _dot_nt(qbd_ref[...], new_ref[...].astype(BF16)) * ATTN_SCALE
    sp = _softplus(z)
    hi, lo = _split_bf16(jnp.where(mask, -sp, 0.0))
    cs = _dot(hi, to_ref[...]) + _dot(lo, to_ref[...])
    a = jnp.where(mask, jnp.exp(z - sp + cs[:, :LANES]), 0.0)
    new_ref[0:4, :] = vn_ref[0]
    oacc_ref[...] = _dot(a.astype(BF16), new_ref[...].astype(BF16))
    acc_ref[...] = cs[:, LANES:]


def _sb_chunk(to_ref, qbd_ref, acc_ref, oacc_ref, k_ref, v_ref, wait_k, wait_v):
    rows = qbd_ref.shape[0]
    tiles = k_ref.shape[-1] // LANES
    wait_k()
    z = _dot(qbd_ref[...], k_ref[...].astype(BF16)) * ATTN_SCALE
    sp = _softplus(z)
    stack = jnp.concatenate([-sp[:, t * LANES:(t + 1) * LANES] for t in range(tiles)], axis=0)
    hi, lo = _split_bf16(stack)
    cs = _dot(hi, to_ref[...]) + _dot(lo, to_ref[...])
    run = acc_ref[...]
    parts = [None] * tiles
    for t in reversed(range(tiles)):
        after = cs[t * rows:(t + 1) * rows, :LANES] + run
        run = run + cs[t * rows:(t + 1) * rows, LANES:]
        sl = slice(t * LANES, (t + 1) * LANES)
        parts[t] = jnp.exp(z[:, sl] - sp[:, sl] + after)
    acc_ref[...] = run
    a = jnp.concatenate(parts, axis=1).astype(BF16)
    wait_v()
    oacc_ref[...] += _dot_nt(a, v_ref[...].astype(BF16))


def _sb_finish(oacc_ref, o_ref):
    rows, width = oacc_ref.shape
    r = lax.broadcasted_iota(I32, (rows, width), 0)
    l = lax.broadcasted_iota(I32, (rows, width), 1)
    diag = jnp.where(l // HEAD_DIM == r % N_HEADS_B, oacc_ref[...], 0.0)
    o_ref[0] = jnp.sum(diag.reshape(4, N_HEADS_B, width), axis=1)


def _sb_sample_kernel(pt_ref, qb_ref, kn_ref, vn_ref, kcache, vcache, to_ref, o_ref,
                      kbuf, vbuf, sem, qbd_ref, acc_ref, oacc_ref, new_ref, *, n_pages, page, chunk):
    n = pl.program_id(0)
    c = pl.program_id(1)
    n_chunks = n_pages * page // chunk
    step = n * n_chunks + c
    slot = step % 2
    pages_per_chunk = chunk // page
    rows = N_HEADS_B * 4
    width = N_HEADS_B * HEAD_DIM
    tiles = chunk // LANES

    def start(st, sl):
        seq = st // n_chunks
        first = n_pages - (st % n_chunks + 1) * pages_per_chunk
        _start_pages(kcache, kbuf.at[sl], sem.at[0, sl], pt_ref, seq, first, pages_per_chunk, page)
        _start_pages(vcache, vbuf.at[sl], sem.at[1, sl], pt_ref, seq, first, pages_per_chunk, page)

    @pl.when(step == 0)
    def _():
        start(step, slot)

    @pl.when(step + 1 < pl.num_programs(0) * n_chunks)
    def _():
        start(step + 1, 1 - slot)

    @pl.when(c == 0)
    def _():
        _sb_new_tokens(qb_ref, kn_ref, vn_ref, to_ref, qbd_ref, acc_ref, oacc_ref, new_ref)

    _sb_chunk(to_ref, qbd_ref, acc_ref, oacc_ref, kbuf.at[slot], vbuf.at[slot],
              lambda: _wait_pages(kcache, kbuf.at[slot], sem.at[0, slot], pages_per_chunk, page),
              lambda: _wait_pages(vcache, vbuf.at[slot], sem.at[1, slot], pages_per_chunk, page))

    @pl.when(c == n_chunks - 1)
    def _():
        _sb_finish(oacc_ref, o_ref)


def _sweep_hooks(s, last, pt_ref, sweep_in, ob_ref, sweep_scratch, n_pages, page, chunk):
    qb_ref, kn_ref, vn_ref, kcache, vcache, to_ref = sweep_in
    kbuf, vbuf, sem, qbd_ref, acc_ref, oacc_ref, new_ref = sweep_scratch
    n_chunks = n_pages * page // chunk
    pages_per_chunk = chunk // page

    def start(seq, c):
        first = n_pages - (c + 1) * pages_per_chunk
        _start_pages(kcache, kbuf.at[c % 2], sem.at[0, c % 2], pt_ref, seq, first, pages_per_chunk, page)
        _start_pages(vcache, vbuf.at[c % 2], sem.at[1, c % 2], pt_ref, seq, first, pages_per_chunk, page)

    def wait(c, which):
        cache, buf = ((kcache, kbuf), (vcache, vbuf))[which]
        _wait_pages(cache, buf.at[c % 2], sem.at[which, c % 2], pages_per_chunk, page)

    def sweep_chunk(c):
        wait(c, 0)
        wait(c, 1)
        if c + 1 < n_chunks:
            start(s, c + 1)
        else:
            start(jnp.minimum(s + 1, last), 0)
        if c == 0:
            _sb_new_tokens(qb_ref, kn_ref, vn_ref, to_ref, qbd_ref, acc_ref, oacc_ref, new_ref)
        _sb_chunk(to_ref, qbd_ref, acc_ref, oacc_ref, kbuf.at[c % 2], vbuf.at[c % 2], lambda: None, lambda: None)
        if c == n_chunks - 1:
            _sb_finish(oacc_ref, ob_ref)

    def prologue():
        @pl.when(s == 0)
        def _():
            start(s, 0)

    def epilogue():
        @pl.when(s == last)
        def _():
            wait(0, 0)
            wait(0, 1)
    return prologue, sweep_chunk, epilogue, n_chunks


def _sweep_chunk_size(page_table, cache):
    past = page_table.shape[1] * cache.shape[-1]
    return SB_CHUNK if past % SB_CHUNK == 0 and past > SB_CHUNK else KEY_CHUNK


def _sweep_specs(chunk, to, step_of):
    width = N_HEADS_B * HEAD_DIM
    rows = N_HEADS_B * 4
    tok = pl.BlockSpec((1, 4, width), lambda *g: (step_of(*g[:-1]), 0, 0))
    in_specs = [tok, tok, tok, pl.BlockSpec(memory_space=pl.ANY), pl.BlockSpec(memory_space=pl.ANY),
                pl.BlockSpec(to.shape, lambda *g: (0, 0), pipeline_mode=pl.Buffered(1))]
    scratch = [pltpu.VMEM((2, width, chunk), F32), pltpu.VMEM((2, width, chunk), F32),
               pltpu.SemaphoreType.DMA((2, 2)), pltpu.VMEM((rows, width), BF16), pltpu.VMEM((rows, LANES), F32),
               pltpu.VMEM((rows, width), F32), pltpu.VMEM((LANES, width), F32)]
    return in_specs, tok, scratch


def _sb_sample_call(page_table, qb4, kb_new, vb_new, cache_k_b, cache_v_b, to):
    n, n_pages = page_table.shape
    page = cache_k_b.shape[-1]
    width = N_HEADS_B * HEAD_DIM
    rows = N_HEADS_B * 4
    tok = pl.BlockSpec((1, 4, width), lambda s, c, pt: (s, 0, 0))
    past = n_pages * page
    chunk = SB_CHUNK if past % SB_CHUNK == 0 and past > SB_CHUNK else KEY_CHUNK
    return pl.pallas_call(
        functools.partial(_sb_sample_kernel, n_pages=n_pages, page=page, chunk=chunk),
        grid_spec=pltpu.PrefetchScalarGridSpec(
            num_scalar_prefetch=1,
            grid=(n, past // chunk),
            in_specs=[tok, tok, tok,
                      pl.BlockSpec(memory_space=pl.ANY), pl.BlockSpec(memory_space=pl.ANY),
                      pl.BlockSpec(to.shape, lambda s, c, pt: (0, 0))],
            out_specs=tok,
            scratch_shapes=[pltpu.VMEM((2, width, chunk), F32),
                            pltpu.VMEM((2, width, chunk), F32),
                            pltpu.SemaphoreType.DMA((2, 2)),
                            pltpu.VMEM((rows, width), BF16),
                            pltpu.VMEM((rows, LANES), F32),
                            pltpu.VMEM((rows, width), F32),
                            pltpu.VMEM((LANES, width), F32)]),
        out_shape=jax.ShapeDtypeStruct((n, 4, width), F32),
        compiler_params=pltpu.CompilerParams(dimension_semantics=("arbitrary", "arbitrary"),
                                             vmem_limit_bytes=VMEM_LIMIT),
        name="sb_sample",
    )(page_table, qb4, kb_new, vb_new, cache_k_b, cache_v_b, to)


def _prep_weights(w_in, w_o_a, w_o_b, w_out, w_up1, w_down1, w_up2, w_down2):
    w = w_in[0]
    d_model = w.shape[0]
    width_a = N_HEADS_A * HEAD_DIM
    kv_a = N_KV_A * HEAD_DIM
    width_i = N_IDX_HEADS * IDX_DIM
    width_b = N_HEADS_B * HEAD_DIM
    sizes = (width_a, kv_a, kv_a, width_i, IDX_DIM, N_IDX_HEADS, width_b, width_b, width_b, d_model, d_model)
    offs = np.cumsum((0,) + sizes)
    q_a, k_a, v_a, q_i, k_i, w_i, q_b, k_b, v_b, g_a, g_b = [w[:, int(offs[n]):int(offs[n + 1])] for n in range(len(sizes))]
    perm = np.concatenate([np.arange(HEAD_DIM) + (g * GROUP_A + j) * HEAD_DIM
                           for j in range(GROUP_A) for g in range(N_KV_A)])
    pad = jnp.zeros((d_model, LANES - IDX_DIM - N_IDX_HEADS), w.dtype)
    proj = [q_a[:, perm], k_a, v_a, q_i, jnp.concatenate([k_i, w_i, pad], axis=1),
            jnp.concatenate([k_i, k_i], axis=1), q_b, k_b, v_b, g_a, g_b]
    bf = lambda a: a.astype(BF16)
    kw = proj[4]
    return dict(
        proj=[bf(p) for p in proj],
        proj_rows=[bf(p) for p in (proj[0], k_a, q_i, proj[5], q_b, k_b, g_a, g_b)],
        proj_cols=bf(jnp.concatenate([k_a, v_a, kw, k_b, v_b], axis=1).T),
        woa=bf(w_o_a[0][perm, :]), wob=bf(w_o_b[0]), wout=bf(w_out[0]),
        wup1=bf(w_up1[0]), wdn1=bf(w_down1[0]), wup2=bf(w_up2[0]), wdn2=bf(w_down2[0]),
    )


def _prompt_tables(rel_bias):
    key = np.arange(LANES)[:, None]
    query = np.arange(Q_TILE)[None, :]
    idx = np.concatenate([_t5_bucket_np(rel * LANES + query - key) for rel in range(3)], axis=0)
    return _bias_call(rel_bias, idx).reshape(rel_bias.shape[1], 3, LANES, Q_TILE)


def _prompt_path(xp, wts, rel_bias, g_ffn1, g_mix, g_ffn2, g_final, batch, seq, sweep):
    assert seq % SUPER == 0
    n_host = batch * (seq // Q_TILE)
    page_table, qb4, kb_new, vb_new, cache_k_b, cache_v_b = sweep
    n_chunks = page_table.shape[1] * cache_k_b.shape[-1] // _sweep_chunk_size(page_table, cache_k_b)
    hosted = None
    if n_host <= page_table.shape[0] and n_chunks % 2 == 0 and n_chunks <= N_DSA_PHASES:
        hosted = (page_table[:n_host], qb4[:n_host], kb_new[:n_host], vb_new[:n_host], cache_k_b, cache_v_b,
                  _tri_ones(False))
    x1 = _ffn_call(xp, g_ffn1, wts["wup1"], wts["wdn1"])
    pr = _proj_prompt_call(x1, g_mix, wts["proj_rows"], wts["proj_cols"], batch, seq)
    oa, ob_head = _dsa_prompt_call(pr, pr["kwt"], pr["vatb"], _prompt_tables(rel_bias), batch, seq, hosted)
    ob = _sb_prompt_call(pr, pr["vbtb"], _tri_lower_ones(), batch, seq)
    y = _post_call(x1, oa, ob, pr["sa"], pr["sb"], wts["woa"], wts["wob"], wts["wout"],
                   g_ffn2, wts["wup2"], wts["wdn2"], g_final)
    return y, pr, ob_head


def _sample_tables(rel_bias):
    t = np.arange(4)[:, None]
    c = np.arange(LANES)[None, :]
    idx = np.concatenate([_t5_bucket_np(LANES + t - c), _t5_bucket_np(t - c)], axis=1)
    bias = _bias_call(rel_bias, idx)
    bias32 = bias.reshape(N_KV_A, GROUP_A, 4, 2 * LANES).transpose(0, 2, 1, 3).reshape(N_HEADS_A * 4, 2 * LANES)
    far = rel_bias[N_BUCKETS - 1].reshape(N_KV_A, 1, GROUP_A, 1)
    cb32 = jnp.broadcast_to(far, (N_KV_A, 4, GROUP_A, 1)).reshape(N_HEADS_A * 4, 1)
    return bias32, cb32


def _key_minor(c):
    nd = c.ndim
    page = c.shape[2]
    c = jnp.transpose(c, (0, 1) + tuple(range(3, nd)) + (2,))
    return c.reshape(c.shape[:2] + (-1, page))


def _sample_proj(xs, wts, g_ffn1, g_mix):
    x1 = _ffn_call(xs, g_ffn1, wts["wup1"], wts["wdn1"])
    return x1, _proj_call(x1, g_mix, wts["proj"])


def _sample_path(x1, pr, ob_head, wts, rel_bias, caches, page_table, g_ffn2, g_final, n_seq, n_tok):
    assert n_tok == 4 and n_seq % SEQ_GROUP == 0
    n_pages = page_table.shape[1]
    cache_k_a, cache_v_a, cache_k_idx, cache_k_b, cache_v_b = caches
    page = cache_k_idx.shape[-1]
    assert (n_pages * page) % KEY_CHUNK == 0 and KEY_CHUNK % page == 0
    topk = min(TOPK_MAX, (n_pages * page + n_tok) // 4)
    half = n_seq // SEQ_GROUP
    qi64 = pr["qi"].reshape(half, SEQ_GROUP, 4, N_IDX_HEADS, IDX_DIM).transpose(0, 3, 1, 2, 4)
    qi64 = qi64.reshape(half * N_IDX_HEADS * SEQ_GROUP * 4, IDX_DIM)
    wcol = pr["kw"][:, IDX_DIM:IDX_DIM + N_IDX_HEADS].reshape(half, SEQ_GROUP, 4, N_IDX_HEADS).transpose(0, 3, 1, 2)
    wcol = wcol.reshape(half * N_IDX_HEADS * SEQ_GROUP * 4, 1)
    ki_new = pr["kw"][:, :IDX_DIM].reshape(n_seq, 4, IDX_DIM)
    sel = _dsa_sample_select_call(page_table, qi64, wcol, ki_new, cache_k_idx, _tri_ones(True), topk)
    bias32, cb32 = _sample_tables(rel_bias)
    oa = _dsa_sample_attn_call(page_table, pr["qa"].reshape(n_seq * 16, LANES), sel, bias32, cb32,
                               pr["ka"].reshape(n_seq, 4, -1), pr["va"].reshape(n_seq, 4, -1), cache_k_a, cache_v_a)
    done = 0 if ob_head is None else ob_head.shape[0]
    parts = [] if ob_head is None else [ob_head]
    if done < n_seq:
        parts.append(_sb_sample_call(page_table[done:], pr["qb"].reshape(n_seq, 4, -1)[done:],
                                     pr["kb"].reshape(n_seq, 4, -1)[done:], pr["vb"].reshape(n_seq, 4, -1)[done:],
                                     cache_k_b, cache_v_b, _tri_ones(False)))
    ob = jnp.concatenate(parts, axis=0)
    oa = oa.reshape(n_seq * 4, -1)
    ob = ob.reshape(n_seq * 4, -1).astype(BF16)
    y = _post_call(x1, oa, ob, pr["sa"], pr["sb"], wts["woa"], wts["wob"], wts["wout"],
                   g_ffn2, wts["wup2"], wts["wdn2"], g_final)
    return y, pr


def _prompt_rows(pr, batch, seq):
    def heads(a, n):
        return a.reshape(batch, n, HEAD_DIM, seq).transpose(0, 3, 1, 2)[None]
    return (heads(pr["kat"], N_KV_A), heads(pr["vat"], N_KV_A),
            pr["kwt"][:, :IDX_DIM, :].transpose(0, 2, 1)[None],
            heads(pr["kbt"], N_HEADS_B), heads(pr["vbt"], N_HEADS_B))


def _rows(pr, lead):
    depth = (1,)
    return (pr["ka"].reshape(depth + lead + (N_KV_A, HEAD_DIM)),
            pr["va"].reshape(depth + lead + (N_KV_A, HEAD_DIM)),
            pr["kw"][:, :IDX_DIM].reshape(depth + lead + (IDX_DIM,)),
            pr["kb"].reshape(depth + lead + (N_HEADS_B, HEAD_DIM)),
            pr["vb"].reshape(depth + lead + (N_HEADS_B, HEAD_DIM)))


def kernel(x_prompt, x_sample, cache_k_a, cache_v_a, cache_k_idx, cache_k_b, cache_v_b, page_table,
           w_in, w_o_a, w_o_b, w_out, rel_bias, g_ffn1, w_up1, w_down1, g_mix, g_ffn2, w_up2, w_down2, g_final):
    assert w_in.shape[0] == 1, "single-layer step"
    batch, seq, d_model = x_prompt.shape
    n_seq, n_tok, _ = x_sample.shape
    assert seq % Q_TILE == 0
    wts = _prep_weights(w_in, w_o_a, w_o_b, w_out, w_up1, w_down1, w_up2, w_down2)
    gf = g_final.reshape(1, d_model)
    caches = [_key_minor(c) for c in (cache_k_a, cache_v_a, cache_k_idx, cache_k_b, cache_v_b)]
    x1s, prs = _sample_proj(x_sample.reshape(n_seq * n_tok, d_model), wts, g_ffn1, g_mix)
    sweep = (page_table, prs["qb"].reshape(n_seq, n_tok, -1), prs["kb"].reshape(n_seq, n_tok, -1),
             prs["vb"].reshape(n_seq, n_tok, -1), caches[3], caches[4])
    yp, prp, ob_head = _prompt_path(x_prompt.reshape(batch * seq, d_model), wts, rel_bias,
                                    g_ffn1, g_mix, g_ffn2, gf, batch, seq, sweep)
    ys, prs = _sample_path(x1s, prs, ob_head, wts, rel_bias, caches, page_table, g_ffn2, gf, n_seq, n_tok)
    return ((yp.reshape(batch, seq, d_model), ys.reshape(n_seq, n_tok, d_model))
            + _prompt_rows(prp, batch, seq) + _rows(prs, (n_seq, n_tok)))
```

```python
import functools
import math

import jax
import jax.numpy as jnp
import numpy as np
from jax import lax
from jax.experimental import pallas as pl
from jax.experimental.pallas import tpu as pltpu

F32 = jnp.float32
BF16 = jnp.bfloat16
I32 = jnp.int32

HEAD_DIM = 64
IDX_DIM = 64
N_HEADS_A = 8
N_KV_A = 2
GROUP_A = N_HEADS_A // N_KV_A
N_IDX_HEADS = 8
N_HEADS_B = 8
TOPK_MAX = 256
N_BUCKETS = 32
MAX_DISTANCE = 128
EPS = 1e-6
ATTN_SCALE = HEAD_DIM ** -0.5
IDX_SCALE = IDX_DIM ** -0.5
IDX_HEAD_SCALE = N_IDX_HEADS ** -0.5

LANES = 128
Q_TILE = 128
NEG = -1e30
INT_MIN = -2 ** 31
VMEM_LIMIT = 56 * 1024 * 1024
FFN_CHUNK = 256
TOKEN_TILE = 512
SUPER = 512
SUB = SUPER // LANES
TL_ROWS = LANES + 16


def _pick_tile(n, pref):
    t = min(n, pref)
    while n % t or t % 8:
        t -= 1
    return t


def _const_spec(shape):
    nd = len(shape)
    return pl.BlockSpec(shape, lambda *_: (0,) * nd, pipeline_mode=pl.Buffered(1))


def _dot(a, b):
    return jnp.dot(a, b, preferred_element_type=F32)


def _dot_nt(a, b):
    return lax.dot_general(a, b, (((1,), (1,)), ((), ())), preferred_element_type=F32)


def _rms(x, g):
    r = lax.rsqrt(jnp.mean(x * x, axis=-1, keepdims=True) + EPS)
    return (x * r) * g


def _softplus(z):
    return jnp.maximum(z, 0.0) + jnp.log(1.0 + jnp.exp(-jnp.abs(z)))


def _split_bf16(x):
    hi = x.astype(BF16)
    lo = (x - hi.astype(F32)).astype(BF16)
    return hi, lo


KEY_NEG_INF = INT_MIN + 0x7FFFFF


def _key_to_float(key):
    key = jnp.maximum(key, KEY_NEG_INF)
    return lax.bitcast_convert_type(key ^ ((key >> 31) & jnp.int32(0x7FFFFFFF)), F32)


def _search_threshold(count_ge, topk, shape):
    def bit_step(it, u):
        bit = lax.shift_left(jnp.int32(1), 31 - it)
        cand = _key_to_float((u | bit) ^ INT_MIN)
        return jnp.where(count_ge(cand) >= topk, u | bit, u)
    u = lax.fori_loop(0, 32, bit_step, jnp.zeros(shape, I32))
    return _key_to_float(u ^ INT_MIN)


def _swiglu(x, g_ref, wup_ref, wdn_ref, act_ref):
    d_ff = wdn_ref.shape[0]
    h = _rms(x, g_ref[...]).astype(BF16)
    for c in range(d_ff // FFN_CHUNK):
        lo = c * FFN_CHUNK
        gate = _dot(h, wup_ref[:, lo:lo + FFN_CHUNK])
        up = _dot(h, wup_ref[:, d_ff + lo:d_ff + lo + FFN_CHUNK])
        act_ref[:, lo:lo + FFN_CHUNK] = (gate * jax.nn.sigmoid(gate) * up).astype(BF16)
    return _dot(act_ref[...], wdn_ref[...])


def _ffn_kernel(x_ref, g_ref, wup_ref, wdn_ref, o_ref, act_ref):
    x = x_ref[...]
    o_ref[...] = x + 0.5 * _swiglu(x, g_ref, wup_ref, wdn_ref, act_ref)


def _ffn_call(x, g, wup, wdn):
    n, d = x.shape
    d_ff = wdn.shape[0]
    tm = _pick_tile(n, TOKEN_TILE)
    row = lambda w: pl.BlockSpec((tm, w), lambda i: (i, 0))
    return pl.pallas_call(
        _ffn_kernel,
        grid=(n // tm,),
        in_specs=[row(d), _const_spec(g.shape), _const_spec(wup.shape), _const_spec(wdn.shape)],
        out_specs=row(d),
        out_shape=jax.ShapeDtypeStruct((n, d), F32),
        scratch_shapes=[pltpu.VMEM((tm, d_ff), BF16)],
        compiler_params=pltpu.CompilerParams(dimension_semantics=("arbitrary",), vmem_limit_bytes=VMEM_LIMIT),
        name="ffn_pre",
    )(x, g, wup, wdn)


def _proj_kernel(x_ref, g_ref, wqa, wka, wva, wqi, wkw, wkk, wqb, wkb, wvb, wga, wgb,
                 qa, ka, va, kab, vab, qi, kw, kk, qb, kb, vb, kbb, vbb, sa, sb):
    h = _rms(x_ref[...], g_ref[...]).astype(BF16)
    qa[...] = _dot(h, wqa[...]).astype(BF16)
    k = _dot(h, wka[...])
    ka[...] = k
    kab[...] = k.astype(BF16)
    v = _dot(h, wva[...])
    va[...] = v
    vab[...] = v.astype(BF16)
    qi[...] = _dot(h, wqi[...]).astype(BF16)
    kw[...] = _dot(h, wkw[...])
    kk[...] = _dot(h, wkk[...]).astype(BF16)
    qb[...] = _dot(h, wqb[...]).astype(BF16)
    k = _dot(h, wkb[...])
    kb[...] = k
    kbb[...] = k.astype(BF16)
    v = _dot(h, wvb[...])
    vb[...] = v
    vbb[...] = v.astype(BF16)
    sa[...] = jax.nn.sigmoid(_dot(h, wga[...]))
    sb[...] = jax.nn.sigmoid(_dot(h, wgb[...]))


_PROJ_OUT = (("qa", 512, BF16), ("ka", 128, F32), ("va", 128, F32), ("kab", 128, BF16), ("vab", 128, BF16),
             ("qi", 512, BF16), ("kw", 128, F32), ("kk", 128, BF16), ("qb", 512, BF16),
             ("kb", 512, F32), ("vb", 512, F32), ("kbb", 512, BF16), ("vbb", 512, BF16),
             ("sa", 1024, F32), ("sb", 1024, F32))


def _proj_call(x, g, weights):
    n, d = x.shape
    tm = _pick_tile(n, TOKEN_TILE)
    row = lambda w: pl.BlockSpec((tm, w), lambda i: (i, 0))
    outs = pl.pallas_call(
        _proj_kernel,
        grid=(n // tm,),
        in_specs=[row(d), _const_spec(g.shape)] + [_const_spec(w.shape) for w in weights],
        out_specs=[row(w) for _, w, _ in _PROJ_OUT],
        out_shape=[jax.ShapeDtypeStruct((n, w), dt) for _, w, dt in _PROJ_OUT],
        compiler_params=pltpu.CompilerParams(dimension_semantics=("arbitrary",), vmem_limit_bytes=VMEM_LIMIT),
        name="proj",
    )(x, g, *weights)
    return dict(zip([nm for nm, _, _ in _PROJ_OUT], outs))


_PROMPT_ROW_OUT = (("qa", 512, BF16), ("kab", 128, BF16), ("qi", 512, BF16), ("kk", 128, BF16),
                   ("qb", 512, BF16), ("kbb", 512, BF16), ("sa", 1024, F32), ("sb", 1024, F32))
_PROMPT_COL_OUT = (("kat", 0, 128), ("vat", 128, 128), ("kwt", 256, 128), ("kbt", 384, 512), ("vbt", 896, 512))


def _proj_prompt_work(x_ref, g_ref, weights, wt_ref, outs):
    wqa, wka, wqi, wkk, wqb, wkb, wga, wgb = weights
    qa, kab, qi, kk, qb, kbb, sa, sb, kat, vat, kwt, kbt, vbt, vatb, vbtb = outs
    h = _rms(x_ref[...], g_ref[...]).astype(BF16)

    def rows_bf16(out, w):
        def item():
            out[...] = _dot(h, w[...]).astype(BF16)
        return item

    def gate(out, w):
        def item():
            out[...] = jax.nn.sigmoid(_dot(h, w[...]))
        return item

    def cols():
        t = _dot_nt(wt_ref[...], h)
        for ref, (_, r0, nr) in zip((kat, vat, kwt, kbt, vbt), _PROMPT_COL_OUT):
            ref[0] = t[r0:r0 + nr]
        vatb[0, 0] = t[128:256].astype(BF16)
        vbtb[0, 0] = t[896:1408].astype(BF16)
    return [rows_bf16(qa, wqa), rows_bf16(kab, wka), rows_bf16(qi, wqi), gate(sa, wga), rows_bf16(kk, wkk),
            rows_bf16(qb, wqb), gate(sb, wgb), rows_bf16(kbb, wkb), cols]


def _proj_prompt_kernel(x_ref, g_ref, *rest):
    for item in _proj_prompt_work(x_ref, g_ref, rest[:8], rest[8], rest[9:]):
        item()


def _proj_prompt_call(x, g, weights, wt_all, batch, seq):
    n, d = x.shape
    tm = SUPER
    per_b = seq // tm
    row = lambda w: pl.BlockSpec((tm, w), lambda i: (i, 0))
    col = lambda r: pl.BlockSpec((1, r, tm), lambda i: (i // per_b, 0, i % per_b))
    blk = lambda r: pl.BlockSpec((1, 1, r, tm), lambda i: (i // per_b, i % per_b, 0, 0))
    in_specs = [row(d), _const_spec(g.shape)] + [_const_spec(w.shape) for w in weights] + [_const_spec(wt_all.shape)]
    out_specs = ([row(w) for _, w, _ in _PROMPT_ROW_OUT] + [col(nr) for _, _, nr in _PROMPT_COL_OUT]
                 + [blk(128), blk(512)])
    out_shape = ([jax.ShapeDtypeStruct((n, w), dt) for _, w, dt in _PROMPT_ROW_OUT]
                 + [jax.ShapeDtypeStruct((batch, nr, seq), F32) for _, _, nr in _PROMPT_COL_OUT]
                 + [jax.ShapeDtypeStruct((batch, per_b, 128, tm), BF16),
                    jax.ShapeDtypeStruct((batch, per_b, 512, tm), BF16)])
    names = [nm for nm, _, _ in _PROMPT_ROW_OUT] + [nm for nm, _, _ in _PROMPT_COL_OUT] + ["vatb", "vbtb"]
    outs = pl.pallas_call(
        _proj_prompt_kernel, grid=(n // tm,), in_specs=in_specs, out_specs=out_specs, out_shape=out_shape,
        compiler_params=pltpu.CompilerParams(dimension_semantics=("arbitrary",), vmem_limit_bytes=VMEM_LIMIT),
        name="proj_prompt",
    )(x, g, *weights, wt_all)
    return dict(zip(names, outs))


def _post_kernel(x_ref, oa_ref, ob_ref, sa_ref, sb_ref, woa, wob, wout, g2_ref, wup_ref, wdn_ref, gf_ref,
                 y_ref, act_ref):
    mix = sa_ref[...] * _dot(oa_ref[...], woa[...]) + sb_ref[...] * _dot(ob_ref[...], wob[...])
    x2 = x_ref[...] + _dot(mix.astype(BF16), wout[...])
    x3 = x2 + 0.5 * _swiglu(x2, g2_ref, wup_ref, wdn_ref, act_ref)
    y_ref[...] = _rms(x3, gf_ref[...])


def _post_call(x, oa, ob, sa, sb, woa, wob, wout, g2, wup, wdn, gf):
    n, d = x.shape
    d_ff = wdn.shape[0]
    tm = _pick_tile(n, TOKEN_TILE)
    row = lambda w: pl.BlockSpec((tm, w), lambda i: (i, 0))
    consts = (woa, wob, wout, g2, wup, wdn, gf)
    return pl.pallas_call(
        _post_kernel,
        grid=(n // tm,),
        in_specs=[row(d), row(oa.shape[1]), row(ob.shape[1]), row(d), row(d)] + [_const_spec(c.shape) for c in consts],
        out_specs=row(d),
        out_shape=jax.ShapeDtypeStruct((n, d), F32),
        scratch_shapes=[pltpu.VMEM((tm, d_ff), BF16)],
        compiler_params=pltpu.CompilerParams(dimension_semantics=("arbitrary",), vmem_limit_bytes=VMEM_LIMIT),
        name="post",
    )(x, oa, ob, sa, sb, *consts)


def _t5_bucket_np(dist):
    max_exact = N_BUCKETS // 2
    d = np.maximum(dist, 0)
    ratio = np.maximum(d, 1).astype(np.float32) / np.float32(max_exact)
    large = max_exact + (np.log(ratio) / np.float32(math.log(MAX_DISTANCE / max_exact))
                         * np.float32(N_BUCKETS - max_exact)).astype(np.int32)
    large = np.minimum(large, N_BUCKETS - 1)
    return np.where(d < max_exact, d, large).astype(np.int32)


def _bias_kernel(rel_ref, idx_ref, out_ref):
    idx = idx_ref[...]
    for h in range(out_ref.shape[0]):
        acc = jnp.zeros(idx.shape, F32)
        for b in range(N_BUCKETS):
            acc = jnp.where(idx == b, rel_ref[b, h], acc)
        out_ref[h] = acc


def _bias_call(rel_bias, idx):
    return pl.pallas_call(
        _bias_kernel,
        in_specs=[pl.BlockSpec(memory_space=pltpu.SMEM), pl.BlockSpec(memory_space=pltpu.VMEM)],
        out_specs=pl.BlockSpec(memory_space=pltpu.VMEM),
        out_shape=jax.ShapeDtypeStruct((rel_bias.shape[1],) + idx.shape, F32),
        name="rel_bias_table",
    )(rel_bias, jnp.asarray(idx))


def _half_masks():
    lane = lax.broadcasted_iota(I32, (Q_TILE, LANES), 1)
    return lane < HEAD_DIM


def _store_masked_pairs(src_ref, dst_ref, n_pairs, scale=1.0):
    lo_half = _half_masks()
    for p in range(n_pairs):
        pair = src_ref[:, p * LANES:(p + 1) * LANES].astype(F32) * scale
        dst_ref[p, 0:Q_TILE, :] = jnp.where(lo_half, pair, 0.0).astype(BF16)
        dst_ref[p, Q_TILE:2 * Q_TILE, :] = jnp.where(lo_half, 0.0, pair).astype(BF16)


def _tri_ones(strict_upper):
    r = np.arange(LANES)
    if strict_upper:
        tri = (r[:, None] < r[None, :])
    else:
        tri = (r[:, None] > r[None, :])
    return jnp.asarray(np.concatenate([tri, np.ones((LANES, LANES), bool)], axis=1), dtype=BF16)


N_DSA_PHASES = 4


def _dsa_prompt_sb_kernel(pt_ref, qi_ref, kwt_ref, kk_ref, qa_ref, ka_ref, vat_ref, bias_ref, lt_ref, *rest,
                          topk, n_pages, page, chunk):
    sweep_in, (o_ref, ob_ref), scratch = rest[:6], rest[6:8], rest[8:]
    s = pl.program_id(0) * pl.num_programs(1) + pl.program_id(1)
    last = pl.num_programs(0) * pl.num_programs(1) - 1
    prologue, sweep_chunk, epilogue, n_chunks = _sweep_hooks(s, last, pt_ref, sweep_in, ob_ref, scratch[8:],
                                                              n_pages, page, chunk)
    at = {(c * N_DSA_PHASES) // n_chunks: c for c in range(n_chunks)}
    prologue()
    _dsa_prompt_kernel(qi_ref, kwt_ref, kk_ref, qa_ref, ka_ref, vat_ref, bias_ref, lt_ref, o_ref, *scratch[:8],
                       topk=topk, phase_hook=lambda ph: sweep_chunk(at[ph]) if ph in at else None)
    epilogue()


def _dsa_prompt_kernel(qi_ref, kwt_ref, kk_ref, qa_ref, ka_ref, vat_ref, bias_ref, lt_ref, o_ref,
                       qim_ref, qam_ref, key_ref, sel_ref, lg_ref, mx_ref, ls_ref, out_ref, *, topk,
                       phase_hook=lambda ph: None):
    i = pl.program_id(1)
    ntile = i + 1
    n_super = i // SUB + 1
    n_pairs = N_HEADS_A // 2
    row = lax.broadcasted_iota(I32, (LANES, LANES), 0)
    lane = lax.broadcasted_iota(I32, (LANES, LANES), 1)
    key_minus_query = row - lane

    phase_hook(0)
    _store_masked_pairs(qi_ref, qim_ref, N_IDX_HEADS // 2, IDX_SCALE)
    _store_masked_pairs(qa_ref, qam_ref, n_pairs, ATTN_SCALE)
    wt = kwt_ref[0, IDX_DIM:IDX_DIM + N_IDX_HEADS, :] * IDX_HEAD_SCALE
    mx_ref[...] = jnp.full(mx_ref.shape, NEG, F32)
    ls_ref[...] = jnp.zeros(ls_ref.shape, F32)
    out_ref[...] = jnp.zeros(out_ref.shape, F32)

    def super_rows(jt):
        return pl.ds(pl.multiple_of(jt * SUPER, SUPER), SUPER)

    def valid_tile(j):
        return key_minus_query <= (i - j) * LANES

    def score_super(jt, c):
        kt = kk_ref[super_rows(jt), :]
        acc = jnp.zeros((SUPER, LANES), F32)
        for p in range(N_IDX_HEADS // 2):
            s = _dot_nt(kt, qim_ref[p])
            acc = acc + wt[2 * p:2 * p + 1] * jnp.maximum(s[:, :LANES], 0.0)
            acc = acc + wt[2 * p + 1:2 * p + 2] * jnp.maximum(s[:, LANES:], 0.0)
        for k in range(SUB):
            j = jt * SUB + k
            key_ref[j] = jnp.where(valid_tile(j), acc[k * LANES:(k + 1) * LANES], -jnp.inf)
        return c
    lax.fori_loop(0, n_super, score_super, 0)
    phase_hook(1)

    def count_tiles(pred):
        def body(jt, cnt):
            for k in range(SUB):
                cnt = cnt + jnp.where(pred(key_ref[jt * SUB + k]), 1, 0)
            return cnt
        cnt = lax.fori_loop(0, n_super, body, jnp.zeros((LANES, LANES), I32))
        return jnp.sum(cnt, axis=0, keepdims=True)

    thr = _search_threshold(lambda cand: count_tiles(lambda k: k >= cand), topk, (1, LANES))
    need = (topk - count_tiles(lambda k: k > thr)).astype(F32)

    over = jnp.logical_and(count_tiles(lambda k: k >= thr) > topk, thr > -jnp.inf)
    ties = jnp.sum(jnp.where(over, 1, 0)) > 0

    @pl.when(ties)
    def _():
        def select_tile(j, carry):
            key = key_ref[j]
            eq = key == thr
            cs = _dot(lt_ref[...], jnp.where(eq, 1.0, 0.0).astype(BF16))
            take = jnp.logical_or(key > thr, jnp.logical_and(eq, cs[:LANES] + carry < need))
            sel_ref[j] = jnp.where(jnp.logical_and(take, valid_tile(j)), 0.0, NEG)
            return carry + cs[LANES:LANES + 1]
        lax.fori_loop(0, n_super * SUB, select_tile, jnp.zeros((1, LANES), F32))

    @pl.when(jnp.logical_not(ties))
    def _():
        def select_tile(j, c):
            take = jnp.logical_and(key_ref[j] >= thr, valid_tile(j))
            sel_ref[j] = jnp.where(take, 0.0, NEG)
            return c
        lax.fori_loop(0, n_super * SUB, select_tile, 0)
    phase_hook(2)

    def logit_super(jt, c):
        kt = ka_ref[super_rows(jt), :]
        for p in range(n_pairs):
            s = _dot_nt(kt, qam_ref[p])
            mx = mx_ref[p]
            for k in range(SUB):
                j = jt * SUB + k
                rel = jnp.clip(i - j, 0, 2)
                sel = sel_ref[j]
                sk = s[k * LANES:(k + 1) * LANES]
                lg0 = sk[:, :LANES] + bias_ref[p, rel] + sel
                lg1 = sk[:, LANES:] + bias_ref[GROUP_A + p, rel] + sel
                lg_ref[p, j, :, 0:LANES] = lg0
                lg_ref[p, j, :, LANES:2 * LANES] = lg1
                lg = jnp.concatenate([lg0, lg1], axis=1)
                mx = jnp.maximum(mx, jnp.max(lg.reshape(LANES // 8, 8, 2 * LANES), axis=0))
            mx_ref[p] = mx
        return c
    lax.fori_loop(0, n_super, logit_super, 0)

    for p in range(n_pairs):
        m = jnp.max(mx_ref[p], axis=0, keepdims=True)
        mx_ref[p] = jnp.broadcast_to(m, mx_ref.shape[1:])
    phase_hook(3)

    def value_super(jt, c):
        vt = vat_ref[0, jt]
        for p in range(n_pairs):
            m = mx_ref[p][0:1]
            parts = []
            ls = ls_ref[p]
            for k in range(SUB):
                pk = jnp.exp(lg_ref[p, jt * SUB + k] - m)
                ls = ls + jnp.sum(pk.reshape(LANES // 8, 8, 2 * LANES), axis=0)
                parts.append(pk.astype(BF16))
            ls_ref[p] = ls
            out_ref[p] += _dot(vt, jnp.concatenate(parts, axis=0))
        return c
    lax.fori_loop(0, n_super, value_super, 0)

    for p in range(n_pairs):
        denom = jnp.sum(ls_ref[p], axis=0, keepdims=True)
        o_ref[:, p * LANES:(p + 1) * LANES] = _pair_transposed_out(out_ref[p] / denom).astype(BF16)


def _tri_prefix_ones():
    r = np.arange(LANES)
    tri = (r[None, :] < r[:, None])
    return jnp.asarray(np.concatenate([tri, np.ones((TL_ROWS - LANES, LANES), bool)], axis=0), dtype=BF16)


def _dsa_prompt_call(pr, kwt, vat, bias3, batch, seq, sweep=None):
    nq = seq // Q_TILE
    n_pairs = N_HEADS_A // 2
    topk = min(TOPK_MAX, seq // 4)
    lt = _tri_prefix_ones()
    ix = (lambda f: (lambda b, i, pt: f(b, i))) if sweep is not None else (lambda f: f)
    qblk = lambda w: pl.BlockSpec((Q_TILE, w), ix(lambda b, i: (b * nq + i, 0)))
    kblk = lambda w: pl.BlockSpec((seq, w), ix(lambda b, i: (b, 0)))
    const = lambda a: pl.BlockSpec(a.shape, ix(lambda b, i: (0,) * a.ndim), pipeline_mode=pl.Buffered(1))
    in_specs = [qblk(512), pl.BlockSpec((1, LANES, Q_TILE), ix(lambda b, i: (b, 0, i))), kblk(LANES),
                qblk(512), kblk(LANES), pl.BlockSpec((1,) + vat.shape[1:], ix(lambda b, i: (b, 0, 0, 0))),
                const(bias3), const(lt)]
    scratch = [
        pltpu.VMEM((N_IDX_HEADS // 2, 2 * Q_TILE, LANES), BF16),
        pltpu.VMEM((n_pairs, 2 * Q_TILE, LANES), BF16),
        pltpu.VMEM((nq, LANES, Q_TILE), F32),
        pltpu.VMEM((nq, LANES, Q_TILE), F32),
        pltpu.VMEM((n_pairs, nq, LANES, 2 * Q_TILE), F32),
        pltpu.VMEM((n_pairs, 8, 2 * Q_TILE), F32),
        pltpu.VMEM((n_pairs, 8, 2 * Q_TILE), F32),
        pltpu.VMEM((n_pairs, LANES, 2 * Q_TILE), F32),
    ]
    out_shape = jax.ShapeDtypeStruct((batch * seq, 512), BF16)
    params = pltpu.CompilerParams(dimension_semantics=("arbitrary", "arbitrary"), vmem_limit_bytes=VMEM_LIMIT)
    args = (pr["qi"], kwt, pr["kk"], pr["qa"], pr["kab"], vat, bias3, lt)
    if sweep is None:
        oa = pl.pallas_call(
            functools.partial(_dsa_prompt_kernel, topk=topk), grid=(batch, nq), in_specs=in_specs,
            out_specs=qblk(512), out_shape=out_shape, scratch_shapes=scratch, compiler_params=params,
            name="dsa_prompt",
        )(*args)
        return oa, None
    page_table, qb4, kb_new, vb_new, cache_k_b, cache_v_b, to = sweep
    n_seq, n_pages = page_table.shape
    assert n_seq == batch * nq
    chunk = _sweep_chunk_size(page_table, cache_k_b)
    sweep_in, tok, sweep_scratch = _sweep_specs(chunk, to, lambda b, i: b * nq + i)
    return pl.pallas_call(
        functools.partial(_dsa_prompt_sb_kernel, topk=topk, n_pages=n_pages, page=cache_k_b.shape[-1], chunk=chunk),
        grid_spec=pltpu.PrefetchScalarGridSpec(
            num_scalar_prefetch=1, grid=(batch, nq), in_specs=in_specs + sweep_in,
            out_specs=[qblk(512), tok], scratch_shapes=scratch + sweep_scratch),
        out_shape=[out_shape, jax.ShapeDtypeStruct((n_seq, 4, N_HEADS_B * HEAD_DIM), F32)],
        compiler_params=params, name="dsa_prompt_sb_sweep",
    )(page_table, *args, qb4, kb_new, vb_new, cache_k_b, cache_v_b, to)


def _tri_lower_ones():
    r = np.arange(LANES)
    tri = (r[None, :] > r[:, None])
    top = np.concatenate([tri, tri], axis=1)
    full = np.concatenate([top, np.ones((TL_ROWS - LANES, 2 * LANES), bool)], axis=0)
    return jnp.asarray(-full.astype(np.float32), dtype=BF16)


def _pair_transposed_out(acc):
    top = lax.broadcasted_iota(I32, (LANES, LANES), 0) < HEAD_DIM
    return jnp.where(top, acc[:, :LANES], acc[:, LANES:]).T


def _sb_prompt_kernel(qb_ref, kb_ref, vbt_ref, tl_ref, o_ref, qm_ref, z_ref, hl_ref, cs_ref, a_ref, run_ref, out_ref):
    i = pl.program_id(1)
    n_pairs = N_HEADS_B // 2
    last = i // SUB

    _store_masked_pairs(qb_ref, qm_ref, n_pairs, ATTN_SCALE)
    run_ref[...] = jnp.zeros(run_ref.shape, F32)
    out_ref[...] = jnp.zeros(out_ref.shape, F32)

    key_row = lax.broadcasted_iota(I32, (SUPER, 2 * LANES), 0)
    q_lane = lax.broadcasted_iota(I32, (SUPER, 2 * LANES), 1) % LANES
    strict = key_row < (i % SUB) * LANES + q_lane

    def super_tile(jt, diag):
        rows = pl.ds(pl.multiple_of(jt * SUPER, SUPER), SUPER)
        for p in range(n_pairs):
            z_ref[p] = _dot_nt(kb_ref[rows, p * LANES:(p + 1) * LANES], qm_ref[p])
        for p in range(n_pairs):
            z = z_ref[p]
            sp = _softplus(z)
            hi, lo = _split_bf16(jnp.where(strict, sp, 0.0) if diag else sp)
            for k in range(SUB):
                ks = slice(k * LANES, (k + 1) * LANES)
                hl_ref[p, :, k * 2 * LANES:(k + 1) * 2 * LANES] = jnp.concatenate([hi[ks], lo[ks]], axis=0)
            z_ref[p] = z - sp
        for p in range(n_pairs):
            cs_ref[p] = _dot(tl_ref[...], hl_ref[p])
        for p in range(n_pairs):
            run = run_ref[p]
            for k in reversed(range(SUB)):
                cs = cs_ref[p, :, k * 2 * LANES:(k + 1) * 2 * LANES]
                ks = slice(k * LANES, (k + 1) * LANES)
                a = jnp.exp(z_ref[p, ks, :] + cs[:LANES] + run)
                if diag:
                    a = jnp.where(strict[ks], a, 0.0)
                a_ref[p, ks, :] = a.astype(BF16)
                run = run + cs[LANES:LANES + 1]
            run_ref[p] = run
        for p in range(n_pairs):
            out_ref[p] += _dot(vbt_ref[0, jt, p * LANES:(p + 1) * LANES, :], a_ref[p])

    super_tile(last, True)

    def older(jj, c):
        super_tile(last - 1 - jj, False)
        return c
    lax.fori_loop(0, last, older, 0)

    for p in range(n_pairs):
        o_ref[:, p * LANES:(p + 1) * LANES] = _pair_transposed_out(out_ref[p]).astype(BF16)


def _sb_prompt_call(pr, vbt, tl, batch, seq):
    nq = seq // Q_TILE
    width = N_HEADS_B * HEAD_DIM
    n_pairs = N_HEADS_B // 2
    qblk = pl.BlockSpec((Q_TILE, width), lambda b, i: (b * nq + i, 0))
    kblk = pl.BlockSpec((seq, width), lambda b, i: (b, 0))
    vblk = pl.BlockSpec((1,) + vbt.shape[1:], lambda b, i: (b, 0, 0, 0))
    return pl.pallas_call(
        _sb_prompt_kernel,
        grid=(batch, nq),
        in_specs=[qblk, kblk, vblk, _const_spec(tl.shape)],
        out_specs=qblk,
        out_shape=jax.ShapeDtypeStruct((batch * seq, width), BF16),
        scratch_shapes=[
            pltpu.VMEM((n_pairs, 2 * Q_TILE, LANES), BF16),
            pltpu.VMEM((n_pairs, SUPER, 2 * LANES), F32),
            pltpu.VMEM((n_pairs, 2 * LANES, SUB * 2 * LANES), BF16),
            pltpu.VMEM((n_pairs, TL_ROWS, SUB * 2 * LANES), F32),
            pltpu.VMEM((n_pairs, SUPER, 2 * LANES), BF16),
            pltpu.VMEM((n_pairs, 1, 2 * LANES), F32),
            pltpu.VMEM((n_pairs, LANES, 2 * LANES), F32),
        ],
        compiler_params=pltpu.CompilerParams(dimension_semantics=("arbitrary", "arbitrary"),
                                             vmem_limit_bytes=VMEM_LIMIT),
        name="sb_prompt",
    )(pr["qb"], pr["kbb"], vbt, tl)


SEQ_GROUP = 4
KEY_CHUNK = 1024
SB_CHUNK = 2048


def _page_copy(cache_ref, buf_ref, sem, phys, p, page):
    return pltpu.make_async_copy(cache_ref.at[0, phys], buf_ref.at[:, p * page:(p + 1) * page], sem)


def _start_pages(cache_ref, buf_ref, sem, pt_ref, seq, first_page, n_copy, page):
    for p in range(n_copy):
        _page_copy(cache_ref, buf_ref, sem, pt_ref[seq, first_page + p], p, page).start(priority=p % 2)


def _wait_pages(cache_ref, buf_ref, sem, n_copy, page):
    for p in range(n_copy):
        _page_copy(cache_ref, buf_ref, sem, 0, p, page).wait()


def _dsa_sample_select_kernel(pt_ref, qi_ref, w_ref, kn_ref, cache_ref, uo_ref, sel_ref,
                              kbuf, sem, key_ref, knew_ref, *, topk, n_pages, page):
    s = pl.program_id(0)
    nstep = pl.num_programs(0)
    slot = s % 2
    past = n_pages * page
    n_tiles = past // LANES + 1
    rows = SEQ_GROUP * 4

    def start(step, sl):
        for g in range(SEQ_GROUP):
            _start_pages(cache_ref, kbuf.at[sl, g], sem.at[sl], pt_ref, step * SEQ_GROUP + g, 0, n_pages, page)

    @pl.when(s == 0)
    def _():
        start(s, slot)

    @pl.when(s + 1 < nstep)
    def _():
        start(s + 1, 1 - slot)

    for g in range(SEQ_GROUP):
        _wait_pages(cache_ref, kbuf.at[slot, g], sem.at[slot], n_pages, page)

    q = qi_ref[...]
    w = w_ref[...] * IDX_HEAD_SCALE
    row = lax.broadcasted_iota(I32, (rows, LANES), 0)
    lane = lax.broadcasted_iota(I32, (rows, LANES), 1)

    def head_sum(sc):
        sc = jnp.maximum(sc * IDX_SCALE, 0.0) * w
        return jnp.sum(sc.reshape(N_IDX_HEADS, rows, sc.shape[-1]), axis=0)

    def own_rows(parts):
        r = lax.broadcasted_iota(I32, parts[0].shape, 0)
        sc = parts[-1]
        for g in reversed(range(SEQ_GROUP - 1)):
            sc = jnp.where(r < 4 * (g + 1), parts[g], sc)
        return sc

    for c in range(past // KEY_CHUNK):
        parts = []
        for g in range(SEQ_GROUP):
            kc = kbuf[slot, g, :, c * KEY_CHUNK:(c + 1) * KEY_CHUNK].astype(BF16)
            parts.append(head_sum(_dot(q, kc)))
        key_ref[:, c * KEY_CHUNK:(c + 1) * KEY_CHUNK] = own_rows(parts)

    parts = []
    for g in range(SEQ_GROUP):
        knew_ref[...] = jnp.zeros(knew_ref.shape, F32)
        knew_ref[0:4, :] = kn_ref[g]
        parts.append(head_sum(_dot_nt(q, knew_ref[...].astype(BF16))))
    sc = own_rows(parts)
    valid_new = jnp.logical_and(lane <= row % 4, lane < 4)
    key_ref[:, past:past + LANES] = jnp.where(valid_new, sc, -jnp.inf)

    key = key_ref[...]
    thr = _search_threshold(lambda cand: jnp.sum(jnp.where(key >= cand, 1, 0), axis=1, keepdims=True), topk, (rows, 1))
    need =(topk - jnp.sum(jnp.where(key > thr, 1, 0), axis=1, keepdims=True)).astype(F32)

    eq = jnp.where(key == thr, 1.0, 0.0)
    stack = jnp.concatenate([eq[:, t * LANES:(t + 1) * LANES] for t in range(n_tiles)], axis=0).astype(BF16)
    cs = _dot(stack, uo_ref[...])
    carry = jnp.zeros((rows, LANES), F32)
    for t in range(n_tiles):
        kt = key[:, t * LANES:(t + 1) * LANES]
        pre = cs[t * rows:(t + 1) * rows, :LANES] + carry
        carry = carry + cs[t * rows:(t + 1) * rows, LANES:]
        take = jnp.logical_or(kt > thr, jnp.logical_and(kt == thr, pre < need))
        if t == n_tiles - 1:
            take = jnp.logical_and(take, valid_new)
        sel_ref[:, t * LANES:(t + 1) * LANES] = jnp.where(take, 0.0, NEG)


def _dsa_sample_select_call(page_table, qi64, wcol, ki_new, cache_k_idx, uo, topk):
    n, n_pages = page_table.shape
    page = cache_k_idx.shape[-1]
    past = n_pages * page
    width = past + LANES
    rows = SEQ_GROUP * 4
    return pl.pallas_call(
        functools.partial(_dsa_sample_select_kernel, topk=topk, n_pages=n_pages, page=page),
        grid_spec=pltpu.PrefetchScalarGridSpec(
            num_scalar_prefetch=1,
            grid=(n // SEQ_GROUP,),
            in_specs=[pl.BlockSpec((N_IDX_HEADS * rows, IDX_DIM), lambda s, pt: (s, 0)),
                      pl.BlockSpec((N_IDX_HEADS * rows, 1), lambda s, pt: (s, 0)),
                      pl.BlockSpec((SEQ_GROUP, 4, IDX_DIM), lambda s, pt: (s, 0, 0)),
                      pl.BlockSpec(memory_space=pl.ANY),
                      pl.BlockSpec(uo.shape, lambda s, pt: (0, 0))],
            out_specs=pl.BlockSpec((rows, width), lambda s, pt: (s, 0)),
            scratch_shapes=[pltpu.VMEM((2, SEQ_GROUP, IDX_DIM, past), F32),
                            pltpu.SemaphoreType.DMA((2,)),
                            pltpu.VMEM((rows, width), F32),
                            pltpu.VMEM((LANES, IDX_DIM), F32)]),
        out_shape=jax.ShapeDtypeStruct((n * 4, width), F32),
        compiler_params=pltpu.CompilerParams(dimension_semantics=("arbitrary",), vmem_limit_bytes=VMEM_LIMIT),
        name="dsa_sample_select",
    )(page_table, qi64, wcol, ki_new, cache_k_idx, uo)


def _dsa_sample_attn_kernel(pt_ref, qa_ref, sel_ref, bias_ref, cb_ref, kn_ref, vn_ref, kcache, vcache, o_ref,
                            kbuf, vbuf, sem, lg_ref, new_ref, *, n_pages, page):
    n = pl.program_id(0)
    nseq = pl.num_programs(0)
    slot = n % 2
    past = n_pages * page
    n_chunks = past // KEY_CHUNK
    rows = N_HEADS_A * 4

    def start(seq, sl):
        _start_pages(kcache, kbuf.at[sl], sem.at[0, sl], pt_ref, seq, 0, n_pages, page)
        _start_pages(vcache, vbuf.at[sl], sem.at[1, sl], pt_ref, seq, 0, n_pages, page)

    @pl.when(n == 0)
    def _():
        start(n, slot)

    @pl.when(n + 1 < nseq)
    def _():
        start(n + 1, 1 - slot)

    lo_half = lax.broadcasted_iota(I32, (rows // 2, LANES), 1) < HEAD_DIM
    q = qa_ref[...].astype(F32)
    q32 = jnp.concatenate([jnp.where(lo_half, q, 0.0), jnp.where(lo_half, 0.0, q)], axis=0).astype(BF16)

    r = lax.broadcasted_iota(I32, (rows, SEQ_GROUP * 4), 0)
    c = lax.broadcasted_iota(I32, (rows, SEQ_GROUP * 4), 1)
    pick = jnp.where(c == (n % SEQ_GROUP) * 4 + (r % 16) // 4, 1.0, 0.0).astype(BF16)
    sel = _dot(pick, sel_ref[...].astype(BF16))

    cb = cb_ref[...]
    bias = bias_ref[...]
    _wait_pages(kcache, kbuf.at[slot], sem.at[0, slot], n_pages, page)
    mx = jnp.full((rows, LANES), NEG, F32)
    for ch in range(n_chunks):
        kc = kbuf[slot, :, ch * KEY_CHUNK:(ch + 1) * KEY_CHUNK].astype(BF16)
        lg = _dot(q32, kc) * ATTN_SCALE + cb + sel[:, ch * KEY_CHUNK:(ch + 1) * KEY_CHUNK]
        if ch == n_chunks - 1:
            near = jnp.concatenate([jnp.zeros((rows, KEY_CHUNK - LANES), F32), bias[:, :LANES] - cb], axis=1)
            lg = lg + near
        lg_ref[ch] = lg
        for t in range(KEY_CHUNK // LANES):
            mx = jnp.maximum(mx, lg[:, t * LANES:(t + 1) * LANES])
    new_ref[...] = jnp.zeros(new_ref.shape, F32)
    new_ref[0:4, :] = kn_ref[0]
    lg_new = _dot_nt(q32, new_ref[...].astype(BF16)) * ATTN_SCALE + bias[:, LANES:] + sel[:, past:]
    m = jnp.max(jnp.maximum(mx, lg_new), axis=1, keepdims=True)

    _wait_pages(vcache, vbuf.at[slot], sem.at[1, slot], n_pages, page)
    new_ref[0:4, :] = vn_ref[0]
    p_new = jnp.exp(lg_new - m)
    out = _dot(p_new.astype(BF16), new_ref[...].astype(BF16))
    lsum = p_new
    for ch in range(n_chunks):
        pr = jnp.exp(lg_ref[ch] - m)
        out = out + _dot_nt(pr.astype(BF16), vbuf[slot, :, ch * KEY_CHUNK:(ch + 1) * KEY_CHUNK].astype(BF16))
        for t in range(KEY_CHUNK // LANES):
            lsum = lsum + pr[:, t * LANES:(t + 1) * LANES]
    out = out / jnp.sum(lsum, axis=1, keepdims=True)
    o_ref[...] = jnp.where(lo_half, out[:rows // 2], out[rows // 2:]).astype(BF16)


def _dsa_sample_attn_call(page_table, qa16, sel, bias32, cb32, ka_new, va_new, cache_k_a, cache_v_a):
    n, n_pages = page_table.shape
    page = cache_k_a.shape[-1]
    past = n_pages * page
    width = past + LANES
    rows = N_HEADS_A * 4
    kv_w = N_KV_A * HEAD_DIM
    return pl.pallas_call(
        functools.partial(_dsa_sample_attn_kernel, n_pages=n_pages, page=page),
        grid_spec=pltpu.PrefetchScalarGridSpec(
            num_scalar_prefetch=1,
            grid=(n,),
            in_specs=[pl.BlockSpec((rows // 2, LANES), lambda s, pt: (s, 0)),
                      pl.BlockSpec((SEQ_GROUP * 4, width), lambda s, pt: (s // SEQ_GROUP, 0)),
                      pl.BlockSpec(bias32.shape, lambda s, pt: (0, 0)),
                      pl.BlockSpec(cb32.shape, lambda s, pt: (0, 0)),
                      pl.BlockSpec((1, 4, kv_w), lambda s, pt: (s, 0, 0)),
                      pl.BlockSpec((1, 4, kv_w), lambda s, pt: (s, 0, 0)),
                      pl.BlockSpec(memory_space=pl.ANY),
                      pl.BlockSpec(memory_space=pl.ANY)],
            out_specs=pl.BlockSpec((rows // 2, LANES), lambda s, pt: (s, 0)),
            scratch_shapes=[pltpu.VMEM((2, kv_w, past), F32),
                            pltpu.VMEM((2, kv_w, past), F32),
                            pltpu.SemaphoreType.DMA((2, 2)),
                            pltpu.VMEM((past // KEY_CHUNK, rows, KEY_CHUNK), F32),
                            pltpu.VMEM((LANES, kv_w), F32)]),
        out_shape=jax.ShapeDtypeStruct((n * rows // 2, LANES), BF16),
        compiler_params=pltpu.CompilerParams(dimension_semantics=("arbitrary",), vmem_limit_bytes=VMEM_LIMIT),
        name="dsa_sample_attn",
    )(page_table, qa16, sel, bias32, cb32, ka_new, va_new, cache_k_a, cache_v_a)


def _sb_new_tokens(qb_ref, kn_ref, vn_ref, to_ref, qbd_ref, acc_ref, oacc_ref, new_ref):
    rows, width = qbd_ref.shape
    q = qb_ref[0].astype(F32)
    q32 = jnp.concatenate([jnp.broadcast_to(q[t:t + 1, :], (N_HEADS_B, width)) for t in range(4)], axis=0)
    r = lax.broadcasted_iota(I32, (rows, width), 0)
    l = lax.broadcasted_iota(I32, (rows, width), 1)
    qbd_ref[...] = jnp.where(l // HEAD_DIM == r % N_HEADS_B, q32, 0.0).astype(BF16)
    row = lax.broadcasted_iota(I32, (rows, LANES), 0)
    lane = lax.broadcasted_iota(I32, (rows, LANES), 1)
    mask = jnp.logical_and(lane < row // N_HEADS_B, lane < 4)
    new_ref[...] = jnp.zeros(new_ref.shape, F32)
    new_ref[0:4, :] = kn_ref[0]
    z = _dot_nt(qbd_ref[...], new_ref[...].astype(BF16)) * ATTN_SCALE
    sp = _softplus(z)
    hi, lo = _split_bf16(jnp.where(mask, -sp, 0.0))
    cs = _dot(hi, to_ref[...]) + _dot(lo, to_ref[...])
    a = jnp.where(mask, jnp.exp(z - sp + cs[:, :LANES]), 0.0)
    new_ref[0:4, :] = vn_ref[0]
    oacc_ref[...] = _dot(a.astype(BF16), new_ref[...].astype(BF16))
    acc_ref[...] = cs[:, LANES:]


def _sb_chunk(to_ref, qbd_ref, acc_ref, oacc_ref, k_ref, v_ref, wait_k, wait_v):
    rows = qbd_ref.shape[0]
    tiles = k_ref.shape[-1] // LANES
    wait_k()
    z = _dot(qbd_ref[...], k_ref[...].astype(BF16)) * ATTN_SCALE
    sp = _softplus(z)
    stack = jnp.concatenate([-sp[:, t * LANES:(t + 1) * LANES] for t in range(tiles)], axis=0)
    hi, lo = _split_bf16(stack)
    cs = _dot(hi, to_ref[...]) + _dot(lo, to_ref[...])
    run = acc_ref[...]
    parts = [None] * tiles
    for t in reversed(range(tiles)):
        after = cs[t * rows:(t + 1) * rows, :LANES] + run
        run = run + cs[t * rows:(t + 1) * rows, LANES:]
        sl = slice(t * LANES, (t + 1) * LANES)
        parts[t] = jnp.exp(z[:, sl] - sp[:, sl] + after)
    acc_ref[...] = run
    a = jnp.concatenate(parts, axis=1).astype(BF16)
    wait_v()
    oacc_ref[...] += _dot_nt(a, v_ref[...].astype(BF16))


def _sb_finish(oacc_ref, o_ref):
    rows, width = oacc_ref.shape
    r = lax.broadcasted_iota(I32, (rows, width), 0)
    l = lax.broadcasted_iota(I32, (rows, width), 1)
    diag = jnp.where(l // HEAD_DIM == r % N_HEADS_B, oacc_ref[...], 0.0)
    o_ref[0] = jnp.sum(diag.reshape(4, N_HEADS_B, width), axis=1)


def _sb_sample_kernel(pt_ref, qb_ref, kn_ref, vn_ref, kcache, vcache, to_ref, o_ref,
                      kbuf, vbuf, sem, qbd_ref, acc_ref, oacc_ref, new_ref, *, n_pages, page, chunk):
    n = pl.program_id(0)
    c = pl.program_id(1)
    n_chunks = n_pages * page // chunk
    step = n * n_chunks + c
    slot = step % 2
    pages_per_chunk = chunk // page
    rows = N_HEADS_B * 4
    width = N_HEADS_B * HEAD_DIM
    tiles = chunk // LANES

    def start(st, sl):
        seq = st // n_chunks
        first = n_pages - (st % n_chunks + 1) * pages_per_chunk
        _start_pages(kcache, kbuf.at[sl], sem.at[0, sl], pt_ref, seq, first, pages_per_chunk, page)
        _start_pages(vcache, vbuf.at[sl], sem.at[1, sl], pt_ref, seq, first, pages_per_chunk, page)

    @pl.when(step == 0)
    def _():
        start(step, slot)

    @pl.when(step + 1 < pl.num_programs(0) * n_chunks)
    def _():
        start(step + 1, 1 - slot)

    @pl.when(c == 0)
    def _():
        _sb_new_tokens(qb_ref, kn_ref, vn_ref, to_ref, qbd_ref, acc_ref, oacc_ref, new_ref)

    _sb_chunk(to_ref, qbd_ref, acc_ref, oacc_ref, kbuf.at[slot], vbuf.at[slot],
              lambda: _wait_pages(kcache, kbuf.at[slot], sem.at[0, slot], pages_per_chunk, page),
              lambda: _wait_pages(vcache, vbuf.at[slot], sem.at[1, slot], pages_per_chunk, page))

    @pl.when(c == n_chunks - 1)
    def _():
        _sb_finish(oacc_ref, o_ref)


def _sweep_hooks(s, last, pt_ref, sweep_in, ob_ref, sweep_scratch, n_pages, page, chunk):
    qb_ref, kn_ref, vn_ref, kcache, vcache, to_ref = sweep_in
    kbuf, vbuf, sem, qbd_ref, acc_ref, oacc_ref, new_ref = sweep_scratch
    n_chunks = n_pages * page // chunk
    pages_per_chunk = chunk // page

    def start(seq, c):
        first = n_pages - (c + 1) * pages_per_chunk
        _start_pages(kcache, kbuf.at[c % 2], sem.at[0, c % 2], pt_ref, seq, first, pages_per_chunk, page)
        _start_pages(vcache, vbuf.at[c % 2], sem.at[1, c % 2], pt_ref, seq, first, pages_per_chunk, page)

    def wait(c, which):
        cache, buf = ((kcache, kbuf), (vcache, vbuf))[which]
        _wait_pages(cache, buf.at[c % 2], sem.at[which, c % 2], pages_per_chunk, page)

    def sweep_chunk(c):
        wait(c, 0)
        wait(c, 1)
        if c + 1 < n_chunks:
            start(s, c + 1)
        else:
            start(jnp.minimum(s + 1, last), 0)
        if c == 0:
            _sb_new_tokens(qb_ref, kn_ref, vn_ref, to_ref, qbd_ref, acc_ref, oacc_ref, new_ref)
        _sb_chunk(to_ref, qbd_ref, acc_ref, oacc_ref, kbuf.at[c % 2], vbuf.at[c % 2], lambda: None, lambda: None)
        if c == n_chunks - 1:
            _sb_finish(oacc_ref, ob_ref)

    def prologue():
        @pl.when(s == 0)
        def _():
            start(s, 0)

    def epilogue():
        @pl.when(s == last)
        def _():
            wait(0, 0)
            wait(0, 1)
    return prologue, sweep_chunk, epilogue, n_chunks


def _sweep_chunk_size(page_table, cache):
    past = page_table.shape[1] * cache.shape[-1]
    return SB_CHUNK if past % SB_CHUNK == 0 and past > SB_CHUNK else KEY_CHUNK


def _sweep_specs(chunk, to, step_of):
    width = N_HEADS_B * HEAD_DIM
    rows = N_HEADS_B * 4
    tok = pl.BlockSpec((1, 4, width), lambda *g: (step_of(*g[:-1]), 0, 0))
    in_specs = [tok, tok, tok, pl.BlockSpec(memory_space=pl.ANY), pl.BlockSpec(memory_space=pl.ANY),
                pl.BlockSpec(to.shape, lambda *g: (0, 0), pipeline_mode=pl.Buffered(1))]
    scratch = [pltpu.VMEM((2, width, chunk), F32), pltpu.VMEM((2, width, chunk), F32),
               pltpu.SemaphoreType.DMA((2, 2)), pltpu.VMEM((rows, width), BF16), pltpu.VMEM((rows, LANES), F32),
               pltpu.VMEM((rows, width), F32), pltpu.VMEM((LANES, width), F32)]
    return in_specs, tok, scratch


def _sb_sample_call(page_table, qb4, kb_new, vb_new, cache_k_b, cache_v_b, to):
    n, n_pages = page_table.shape
    page = cache_k_b.shape[-1]
    width = N_HEADS_B * HEAD_DIM
    rows = N_HEADS_B * 4
    tok = pl.BlockSpec((1, 4, width), lambda s, c, pt: (s, 0, 0))
    past = n_pages * page
    chunk = SB_CHUNK if past % SB_CHUNK == 0 and past > SB_CHUNK else KEY_CHUNK
    return pl.pallas_call(
        functools.partial(_sb_sample_kernel, n_pages=n_pages, page=page, chunk=chunk),
        grid_spec=pltpu.PrefetchScalarGridSpec(
            num_scalar_prefetch=1,
            grid=(n, past // chunk),
            in_specs=[tok, tok, tok,
                      pl.BlockSpec(memory_space=pl.ANY), pl.BlockSpec(memory_space=pl.ANY),
                      pl.BlockSpec(to.shape, lambda s, c, pt: (0, 0))],
            out_specs=tok,
            scratch_shapes=[pltpu.VMEM((2, width, chunk), F32),
                            pltpu.VMEM((2, width, chunk), F32),
                            pltpu.SemaphoreType.DMA((2, 2)),
                            pltpu.VMEM((rows, width), BF16),
                            pltpu.VMEM((rows, LANES), F32),
                            pltpu.VMEM((rows, width), F32),
                            pltpu.VMEM((LANES, width), F32)]),
        out_shape=jax.ShapeDtypeStruct((n, 4, width), F32),
        compiler_params=pltpu.CompilerParams(dimension_semantics=("arbitrary", "arbitrary"),
                                             vmem_limit_bytes=VMEM_LIMIT),
        name="sb_sample",
    )(page_table, qb4, kb_new, vb_new, cache_k_b, cache_v_b, to)


def _prep_weights(w_in, w_o_a, w_o_b, w_out, w_up1, w_down1, w_up2, w_down2):
    w = w_in[0]
    d_model = w.shape[0]
    width_a = N_HEADS_A * HEAD_DIM
    kv_a = N_KV_A * HEAD_DIM
    width_i = N_IDX_HEADS * IDX_DIM
    width_b = N_HEADS_B * HEAD_DIM
    sizes = (width_a, kv_a, kv_a, width_i, IDX_DIM, N_IDX_HEADS, width_b, width_b, width_b, d_model, d_model)
    offs = np.cumsum((0,) + sizes)
    q_a, k_a, v_a, q_i, k_i, w_i, q_b, k_b, v_b, g_a, g_b = [w[:, int(offs[n]):int(offs[n + 1])] for n in range(len(sizes))]
    perm = np.concatenate([np.arange(HEAD_DIM) + (g * GROUP_A + j) * HEAD_DIM
                           for j in range(GROUP_A) for g in range(N_KV_A)])
    pad = jnp.zeros((d_model, LANES - IDX_DIM - N_IDX_HEADS), w.dtype)
    proj = [q_a[:, perm], k_a, v_a, q_i, jnp.concatenate([k_i, w_i, pad], axis=1),
            jnp.concatenate([k_i, k_i], axis=1), q_b, k_b, v_b, g_a, g_b]
    bf = lambda a: a.astype(BF16)
    kw = proj[4]
    return dict(
        proj=[bf(p) for p in proj],
        proj_rows=[bf(p) for p in (proj[0], k_a, q_i, proj[5], q_b, k_b, g_a, g_b)],
        proj_cols=bf(jnp.concatenate([k_a, v_a, kw, k_b, v_b], axis=1).T),
        woa=bf(w_o_a[0][perm, :]), wob=bf(w_o_b[0]), wout=bf(w_out[0]),
        wup1=bf(w_up1[0]), wdn1=bf(w_down1[0]), wup2=bf(w_up2[0]), wdn2=bf(w_down2[0]),
    )


def _prompt_tables(rel_bias):
    key = np.arange(LANES)[:, None]
    query = np.arange(Q_TILE)[None, :]
    idx = np.concatenate([_t5_bucket_np(rel * LANES + query - key) for rel in range(3)], axis=0)
    return _bias_call(rel_bias, idx).reshape(rel_bias.shape[1], 3, LANES, Q_TILE)


def _prompt_path(xp, wts, rel_bias, g_ffn1, g_mix, g_ffn2, g_final, batch, seq, sweep):
    assert seq % SUPER == 0
    n_host = batch * (seq // Q_TILE)
    page_table, qb4, kb_new, vb_new, cache_k_b, cache_v_b = sweep
    n_chunks = page_table.shape[1] * cache_k_b.shape[-1] // _sweep_chunk_size(page_table, cache_k_b)
    hosted = None
    if n_host <= page_table.shape[0] and n_chunks % 2 == 0 and n_chunks <= N_DSA_PHASES:
        hosted = (page_table[:n_host], qb4[:n_host], kb_new[:n_host], vb_new[:n_host], cache_k_b, cache_v_b,
                  _tri_ones(False))
    x1 = _ffn_call(xp, g_ffn1, wts["wup1"], wts["wdn1"])
    pr = _proj_prompt_call(x1, g_mix, wts["proj_rows"], wts["proj_cols"], batch, seq)
    oa, ob_head = _dsa_prompt_call(pr, pr["kwt"], pr["vatb"], _prompt_tables(rel_bias), batch, seq, hosted)
    ob = _sb_prompt_call(pr, pr["vbtb"], _tri_lower_ones(), batch, seq)
    y = _post_call(x1, oa, ob, pr["sa"], pr["sb"], wts["woa"], wts["wob"], wts["wout"],
                   g_ffn2, wts["wup2"], wts["wdn2"], g_final)
    return y, pr, ob_head


def _sample_tables(rel_bias):
    t = np.arange(4)[:, None]
    c = np.arange(LANES)[None, :]
    idx = np.concatenate([_t5_bucket_np(LANES + t - c), _t5_bucket_np(t - c)], axis=1)
    bias = _bias_call(rel_bias, idx)
    bias32 = bias.reshape(N_KV_A, GROUP_A, 4, 2 * LANES).transpose(0, 2, 1, 3).reshape(N_HEADS_A * 4, 2 * LANES)
    far = rel_bias[N_BUCKETS - 1].reshape(N_KV_A, 1, GROUP_A, 1)
    cb32 = jnp.broadcast_to(far, (N_KV_A, 4, GROUP_A, 1)).reshape(N_HEADS_A * 4, 1)
    return bias32, cb32


def _key_minor(c):
    nd = c.ndim
    page = c.shape[2]
    c = jnp.transpose(c, (0, 1) + tuple(range(3, nd)) + (2,))
    return c.reshape(c.shape[:2] + (-1, page))


def _sample_proj(xs, wts, g_ffn1, g_mix):
    x1 = _ffn_call(xs, g_ffn1, wts["wup1"], wts["wdn1"])
    return x1, _proj_call(x1, g_mix, wts["proj"])


def _sample_path(x1, pr, ob_head, wts, rel_bias, caches, page_table, g_ffn2, g_final, n_seq, n_tok):
    assert n_tok == 4 and n_seq % SEQ_GROUP == 0
    n_pages = page_table.shape[1]
    cache_k_a, cache_v_a, cache_k_idx, cache_k_b, cache_v_b = caches
    page = cache_k_idx.shape[-1]
    assert (n_pages * page) % KEY_CHUNK == 0 and KEY_CHUNK % page == 0
    topk = min(TOPK_MAX, (n_pages * page + n_tok) // 4)
    half = n_seq // SEQ_GROUP
    qi64 = pr["qi"].reshape(half, SEQ_GROUP, 4, N_IDX_HEADS, IDX_DIM).transpose(0, 3, 1, 2, 4)
    qi64 = qi64.reshape(half * N_IDX_HEADS * SEQ_GROUP * 4, IDX_DIM)
    wcol = pr["kw"][:, IDX_DIM:IDX_DIM + N_IDX_HEADS].reshape(half, SEQ_GROUP, 4, N_IDX_HEADS).transpose(0, 3, 1, 2)
    wcol = wcol.reshape(half * N_IDX_HEADS * SEQ_GROUP * 4, 1)
    ki_new = pr["kw"][:, :IDX_DIM].reshape(n_seq, 4, IDX_DIM)
    sel = _dsa_sample_select_call(page_table, qi64, wcol, ki_new, cache_k_idx, _tri_ones(True), topk)
    bias32, cb32 = _sample_tables(rel_bias)
    oa = _dsa_sample_attn_call(page_table, pr["qa"].reshape(n_seq * 16, LANES), sel, bias32, cb32,
                               pr["ka"].reshape(n_seq, 4, -1), pr["va"].reshape(n_seq, 4, -1), cache_k_a, cache_v_a)
    done = 0 if ob_head is None else ob_head.shape[0]
    parts = [] if ob_head is None else [ob_head]
    if done < n_seq:
        parts.append(_sb_sample_call(page_table[done:], pr["qb"].reshape(n_seq, 4, -1)[done:],
                                     pr["kb"].reshape(n_seq, 4, -1)[done:], pr["vb"].reshape(n_seq, 4, -1)[done:],
                                     cache_k_b, cache_v_b, _tri_ones(False)))
    ob = jnp.concatenate(parts, axis=0)
    oa = oa.reshape(n_seq * 4, -1)
    ob = ob.reshape(n_seq * 4, -1).astype(BF16)
    y = _post_call(x1, oa, ob, pr["sa"], pr["sb"], wts["woa"], wts["wob"], wts["wout"],
                   g_ffn2, wts["wup2"], wts["wdn2"], g_final)
    return y, pr


def _prompt_rows(pr, batch, seq):
    def heads(a, n):
        return a.reshape(batch, n, HEAD_DIM, seq).transpose(0, 3, 1, 2)[None]
    return (heads(pr["kat"], N_KV_A), heads(pr["vat"], N_KV_A),
            pr["kwt"][:, :IDX_DIM, :].transpose(0, 2, 1)[None],
            heads(pr["kbt"], N_HEADS_B), heads(pr["vbt"], N_HEADS_B))


def _rows(pr, lead):
    depth = (1,)
    return (pr["ka"].reshape(depth + lead + (N_KV_A, HEAD_DIM)),
            pr["va"].reshape(depth + lead + (N_KV_A, HEAD_DIM)),
            pr["kw"][:, :IDX_DIM].reshape(depth + lead + (IDX_DIM,)),
            pr["kb"].reshape(depth + lead + (N_HEADS_B, HEAD_DIM)),
            pr["vb"].reshape(depth + lead + (N_HEADS_B, HEAD_DIM)))


def kernel(x_prompt, x_sample, cache_k_a, cache_v_a, cache_k_idx, cache_k_b, cache_v_b, page_table,
           w_in, w_o_a, w_o_b, w_out, rel_bias, g_ffn1, w_up1, w_down1, g_mix, g_ffn2, w_up2, w_down2, g_final):
    assert w_in.shape[0] == 1, "single-layer step"
    batch, seq, d_model = x_prompt.shape
    n_seq, n_tok, _ = x_sample.shape
    assert seq % Q_TILE == 0
    wts = _prep_weights(w_in, w_o_a, w_o_b, w_out, w_up1, w_down1, w_up2, w_down2)
    gf = g_final.reshape(1, d_model)
    caches = [_key_minor(c) for c in (cache_k_a, cache_v_a, cache_k_idx, cache_k_b, cache_v_b)]
    x1s, prs = _sample_proj(x_sample.reshape(n_seq * n_tok, d_model), wts, g_ffn1, g_mix)
    sweep = (page_table, prs["qb"].reshape(n_seq, n_tok, -1), prs["kb"].reshape(n_seq, n_tok, -1),
             prs["vb"].reshape(n_seq, n_tok, -1), caches[3], caches[4])
    yp, prp, ob_head = _prompt_path(x_prompt.reshape(batch * seq, d_model), wts, rel_bias,
                                    g_ffn1, g_mix, g_ffn2, gf, batch, seq, sweep)
    ys, prs = _sample_path(x1s, prs, ob_head, wts, rel_bias, caches, page_table, g_ffn2, gf, n_seq, n_tok)
    return ((yp.reshape(batch, seq, d_model), ys.reshape(n_seq, n_tok, d_model))
            + _prompt_rows(prp, batch, seq) + _rows(prs, (n_seq, n_tok)))
```
